```python
import jax
import jax.numpy as jnp
from jax import lax
import numpy as np

D_MODEL = 2048
BATCH = 8
SEQ = 2048
DEPTH = 2
DEC_BATCH = 128
DEC_SEQ = 4
PAST_LEN = 8192
PAGE_SIZE = 128

GLA_HEADS = 4
GLA_DK = D_MODEL // 2
GLA_DV = D_MODEL
GLA_HEAD_K = GLA_DK // GLA_HEADS
GLA_HEAD_V = GLA_DV // GLA_HEADS
GATE_RANK = 16
GATE_TEMP = 16.0
GLA_CHUNK = 64
SWA_HD = 64
SWA_HQ = D_MODEL // SWA_HD
SWA_KV = SWA_HQ // 8
SWA_GROUP = SWA_HQ // SWA_KV
WINDOW = 128
PEER_HEADS = 8
PEER_QDIM = 256
PEER_HALF = PEER_QDIM // 2
N_KEYS = 128
N_EXPERTS = N_KEYS * N_KEYS
PEER_TOPK = 16
PEER_BLOCK = 128
RMS_EPS = 1e-6
IN_WIDTHS = (GLA_DK, GLA_DK, GLA_DV, GLA_DV, GATE_RANK,
             SWA_HQ * SWA_HD, SWA_KV * SWA_HD, SWA_KV * SWA_HD,
             D_MODEL, D_MODEL)
IN_WIDTH = sum(IN_WIDTHS)

kernel_name = 'hybrid_gla_swa_peer_adaln_step'


def rmsnorm(x, g):
    xf = x.astype(jnp.float32)
    xf = xf * lax.rsqrt(jnp.mean(xf * xf, axis=-1, keepdims=True) + RMS_EPS)
    return xf.astype(x.dtype) * g


def gla_scan(q, k, v, log_a, s0, chunk):
    B, L, H, _ = q.shape
    n = L // chunk

    def chunks(t):
        return t.astype(jnp.float32).reshape(B, n, chunk, H, t.shape[-1]).transpose(1, 0, 3, 2, 4)

    causal = jnp.tril(jnp.ones((chunk, chunk), dtype=bool))

    def step(s, inp):
        qc, kc, vc, gc = inp
        b = lax.cumsum(gc, axis=2)
        rel = jnp.where(causal[:, :, None], b[:, :, :, None, :] - b[:, :, None, :, :], -jnp.inf)
        att = jnp.einsum('bhcd,bhsd,bhcsd->bhcs', qc, kc, jnp.exp(rel))
        o = jnp.einsum('bhcd,bhde->bhce', qc * jnp.exp(b), s) + jnp.einsum('bhcs,bhse->bhce', att, vc)
        b_end = b[:, :, -1:, :]
        s_new = s * jnp.exp(b_end)[:, :, 0, :, None] + jnp.einsum('bhsd,bhse->bhde', kc * jnp.exp(b_end - b), vc)
        return s_new, o

    s_fin, o = lax.scan(step, s0, (chunks(q), chunks(k), chunks(v), chunks(log_a)))
    o = o.transpose(1, 0, 3, 2, 4).reshape(B, L, H, v.shape[-1])
    return o, s_fin


def sink_attention(q, k, v, mask, sinks):
    s = jnp.einsum('...qkgd,...skd->...kgqs', q, k).astype(jnp.float32) * (SWA_HD ** -0.5)
    s = jnp.where(mask, s, -jnp.inf)
    sink = sinks.astype(jnp.float32).reshape(SWA_KV, SWA_GROUP, 1, 1)
    m = jnp.maximum(jnp.max(s, axis=-1, keepdims=True), sink)
    p = jnp.exp(s - m)
    p = p / (jnp.sum(p, axis=-1, keepdims=True) + jnp.exp(sink - m))
    return jnp.einsum('...kgqs,...skd->...qkgd', p.astype(v.dtype), v)


def swa_banded(q, k, v, sinks):
    B, L = q.shape[0], q.shape[1]
    n = L // WINDOW
    qb = q.reshape(B, n, WINDOW, SWA_KV, SWA_GROUP, SWA_HD).transpose(1, 0, 2, 3, 4, 5)

    def band(t):
        tb = t.reshape(B, n, WINDOW, SWA_KV, SWA_HD).transpose(1, 0, 2, 3, 4)
        prev = jnp.concatenate([jnp.zeros_like(tb[:1]), tb[:-1]], axis=0)
        return jnp.concatenate([prev, tb], axis=2)

    kb, vb = band(k), band(v)
    d = (jnp.arange(WINDOW)[:, None] + WINDOW) - jnp.arange(2 * WINDOW)[None, :]
    base = (d >= 0) & (d < WINDOW)
    real = (jnp.arange(n)[:, None, None] > 0) | (jnp.arange(2 * WINDOW)[None, None, :] >= WINDOW)
    masks = base[None] & real
    o = lax.map(lambda a: sink_attention(a[0], a[1], a[2], a[3], sinks), (qb, kb, vb, masks))
    return o.transpose(1, 0, 2, 3, 4, 5).reshape(B, L, SWA_HQ * SWA_HD)


def token_mixer(h, gla_s0, swa_k_past, swa_v_past, w_in, w_alpha2, b_alpha, gla_norm_g, sinks, w_out):
    B, L, _ = h.shape
    f32 = jnp.float32
    gq, gk, gv, gr, glr, sq, sk, sv, ga, gb = jnp.split(h @ w_in, np.cumsum(IN_WIDTHS)[:-1].tolist(), axis=-1)

    log_a = jax.nn.log_sigmoid((glr @ w_alpha2 + b_alpha).astype(f32)) / GATE_TEMP
    hk = lambda t: t.reshape(B, L, GLA_HEADS, GLA_HEAD_K)
    if gla_s0 is None:
        s0 = jnp.zeros((B, GLA_HEADS, GLA_HEAD_K, GLA_HEAD_V), f32)
        chunk = min(GLA_CHUNK, L)
    else:
        s0 = gla_s0.astype(f32)
        chunk = L
    o_gla, s_gla = gla_scan(hk(gq) * (GLA_HEAD_K ** -0.5), hk(gk), gv.reshape(B, L, GLA_HEADS, GLA_HEAD_V),
                            hk(log_a), s0, chunk)
    o_gla = rmsnorm(o_gla, gla_norm_g.astype(f32)) * jax.nn.silu(gr.reshape(B, L, GLA_HEADS, GLA_HEAD_V).astype(f32))
    o_gla = o_gla.astype(h.dtype).reshape(B, L, GLA_DV)

    q = sq.reshape(B, L, SWA_KV, SWA_GROUP, SWA_HD)
    k = sk.reshape(B, L, SWA_KV, SWA_HD)
    v = sv.reshape(B, L, SWA_KV, SWA_HD)
    if swa_k_past is None:
        o_swa = swa_banded(q, k, v, sinks)
        k_buf, v_buf = k[:, -WINDOW:], v[:, -WINDOW:]
    else:
        P = swa_k_past.shape[1]
        kk = jnp.concatenate([swa_k_past.astype(k.dtype), k], axis=1)
        vv = jnp.concatenate([swa_v_past.astype(v.dtype), v], axis=1)
        d = jnp.arange(L)[:, None] - jnp.arange(-P, L)[None, :]
        mask = (d >= 0) & (d < WINDOW)
        o_swa = sink_attention(q, kk, vv, mask, sinks).reshape(B, L, SWA_HQ * SWA_HD)
        k_buf, v_buf = kk[:, -P:], vv[:, -P:]

    merged = jax.nn.sigmoid(ga) * o_gla + jax.nn.sigmoid(gb) * o_swa
    return merged @ w_out, s_gla.astype(h.dtype), k_buf, v_buf


def peer_ffn(h, wq, sub_keys, u, v):
    lead = h.shape[:-1]
    x = h.reshape(-1, D_MODEL)
    T = x.shape[0]
    q = (x @ wq).astype(jnp.float32).reshape(T, PEER_HEADS, 2, PEER_HALF)
    s = jnp.einsum('thcd,hcnd->thcn', q, sub_keys.astype(jnp.float32))
    top_s, top_i = lax.top_k(s, PEER_TOPK)
    comb = (top_s[:, :, 0, :, None] + top_s[:, :, 1, None, :]).reshape(T, PEER_HEADS, PEER_TOPK * PEER_TOPK)
    best_s, best_j = lax.top_k(comb, PEER_TOPK)
    i1 = jnp.take_along_axis(top_i[:, :, 0], best_j // PEER_TOPK, axis=-1)
    i2 = jnp.take_along_axis(top_i[:, :, 1], best_j % PEER_TOPK, axis=-1)
    idx = (i1 * N_KEYS + i2).reshape(T, PEER_HEADS * PEER_TOPK)
    gate = jax.nn.softmax(best_s, axis=-1).reshape(T, PEER_HEADS * PEER_TOPK).astype(h.dtype)
    n_blk = -(-T // PEER_BLOCK)
    pad = n_blk * PEER_BLOCK - T

    def blocks(t):
        t = jnp.pad(t, ((0, pad),) + ((0, 0),) * (t.ndim - 1))
        return t.reshape((n_blk, PEER_BLOCK) + t.shape[1:])

    def expert_block(args):
        xb, ib, gblk = args
        act = jax.nn.gelu(jnp.einsum('td,ted->te', xb, jnp.take(u, ib, axis=0)), approximate=False)
        return jnp.einsum('te,ted->td', gblk * act, jnp.take(v, ib, axis=0))

    out = lax.map(expert_block, (blocks(x), blocks(idx), blocks(gate)))
    return out.reshape(n_blk * PEER_BLOCK, D_MODEL)[:T].reshape(lead + (D_MODEL,))


def decoder_layer(x, c, gla_s0, swa_k_past, swa_v_past, w_ada, b_ada, norm1_g, norm2_g, w_in, w_alpha2,
                  b_alpha, gla_norm_g, sinks, w_out, peer_wq, peer_keys, peer_u, peer_v):
    mod = (jax.nn.silu(c) @ w_ada + b_ada)[:, None, :]
    sh1, sc1, gt1, sh2, sc2, gt2 = jnp.split(mod, 6, axis=-1)
    h = rmsnorm(x, norm1_g) * (1 + sc1) + sh1
    y, s_gla, k_buf, v_buf = token_mixer(h, gla_s0, swa_k_past, swa_v_past, w_in, w_alpha2, b_alpha,
                                         gla_norm_g, sinks, w_out)
    x = x + gt1 * y
    h = rmsnorm(x, norm2_g) * (1 + sc2) + sh2
    x = x + gt2 * peer_ffn(h, peer_wq, peer_keys, peer_u, peer_v)
    return x, s_gla, k_buf, v_buf


def setup_inputs(seed: int = 0) -> dict:
    key = jax.random.key(seed)
    ks = jax.random.split(key, 24)
    f32 = jnp.float32
    nrm = lambda k, shape, s: jax.random.normal(k, shape, f32) * s
    win_buf = min(WINDOW, PAST_LEN)
    return {
        'x_prompt': nrm(ks[0], (BATCH, SEQ, D_MODEL), 1.0),
        'x_sample': nrm(ks[1], (DEC_BATCH, DEC_SEQ, D_MODEL), 1.0),
        'state_gla': nrm(ks[2], (DEPTH, DEC_BATCH, GLA_HEADS, GLA_HEAD_K, GLA_HEAD_V), 1.0),
        'cache_swa_k': nrm(ks[3], (DEPTH, DEC_BATCH, win_buf, SWA_KV, SWA_HD), 1.0),
        'cache_swa_v': nrm(ks[4], (DEPTH, DEC_BATCH, win_buf, SWA_KV, SWA_HD), 1.0),
        'c_prompt': nrm(ks[5], (BATCH, D_MODEL), 1.0),
        'c_sample': nrm(ks[6], (DEC_BATCH, D_MODEL), 1.0),
        'w_ada': nrm(ks[7], (DEPTH, D_MODEL, 6 * D_MODEL), 0.3 * D_MODEL ** -0.5),
        'b_ada': nrm(ks[8], (DEPTH, 6 * D_MODEL), 0.02),
        'norm1_g': 1.0 + nrm(ks[9], (DEPTH, D_MODEL), 0.02),
        'norm2_g': 1.0 + nrm(ks[10], (DEPTH, D_MODEL), 0.02),
        'w_in': nrm(ks[11], (DEPTH, D_MODEL, IN_WIDTH), D_MODEL ** -0.5),
        'w_alpha2': nrm(ks[12], (DEPTH, GATE_RANK, GLA_DK), GATE_RANK ** -0.5),
        'b_alpha': nrm(ks[13], (DEPTH, GLA_DK), 0.1),
        'gla_norm_g': 1.0 + nrm(ks[14], (DEPTH, GLA_HEAD_V), 0.02),
        'swa_sinks': nrm(ks[15], (DEPTH, SWA_HQ), 1.0),
        'w_out': nrm(ks[16], (DEPTH, D_MODEL, D_MODEL), D_MODEL ** -0.5),
        'peer_wq': nrm(ks[17], (DEPTH, D_MODEL, PEER_HEADS * PEER_QDIM), D_MODEL ** -0.5),
        'peer_keys': nrm(ks[18], (DEPTH, PEER_HEADS, 2, N_KEYS, PEER_HALF), PEER_HALF ** -0.5),
        'peer_u': nrm(ks[19], (DEPTH, N_EXPERTS, D_MODEL), D_MODEL ** -0.5),
        'peer_v': nrm(ks[20], (DEPTH, N_EXPERTS, D_MODEL), 1.0),
        'final_g': 1.0 + nrm(ks[21], (D_MODEL,), 0.02),
    }


def reference(x_prompt, x_sample, state_gla, cache_swa_k, cache_swa_v, c_prompt, c_sample, w_ada, b_ada,
              norm1_g, norm2_g, w_in, w_alpha2, b_alpha, gla_norm_g, swa_sinks, w_out, peer_wq, peer_keys,
              peer_u, peer_v, final_g):
    xp, xs = x_prompt, x_sample
    gla_p, kp, vp, gla_s, ksm, vsm = [], [], [], [], [], []
    for l in range(DEPTH):
        w = (w_ada[l], b_ada[l], norm1_g[l], norm2_g[l], w_in[l], w_alpha2[l], b_alpha[l], gla_norm_g[l],
             swa_sinks[l], w_out[l], peer_wq[l], peer_keys[l], peer_u[l], peer_v[l])
        xp, s_p, k_p, v_p = decoder_layer(xp, c_prompt, None, None, None, *w)
        xs, s_s, k_s, v_s = decoder_layer(xs, c_sample, state_gla[l], cache_swa_k[l], cache_swa_v[l], *w)
        gla_p.append(s_p); kp.append(k_p); vp.append(v_p)
        gla_s.append(s_s); ksm.append(k_s); vsm.append(v_s)
    y_prompt = rmsnorm(xp, final_g)
    y_sample = rmsnorm(xs, final_g)
    return (y_prompt, y_sample, jnp.stack(gla_p), jnp.stack(kp), jnp.stack(vp),
            jnp.stack(gla_s), jnp.stack(ksm), jnp.stack(vsm))
```

```python
import functools
from typing import NamedTuple

import numpy as np
import jax
import jax.numpy as jnp
from jax import lax
from jax.experimental import pallas as pl
from jax.experimental.pallas import tpu as pltpu

F32 = jnp.float32
BF16 = jnp.bfloat16

D = 2048
DEPTH = 2
GLA_H, GLA_DK, GLA_DV, GATE_RANK, GATE_TEMP = 4, 256, 512, 16, 16.0
SWA_HD, SWA_KV, SWA_G, WINDOW = 64, 4, 8, 128
PEER_H, N_KEYS, TOPK = 8, 128, 16
N_EXPERTS = N_KEYS * N_KEYS
EPS = 1e-6

COL_GQ, COL_GK, COL_GV, COL_GR = 0, 1024, 2048, 4096
COL_SQ, COL_GA, COL_GB = 6144, 8192, 10240
COL_SK, COL_SV, COL_GLR = 12288, 12544, 12800
N_PROJ = 13312
GLR_BLOCK = 128

MIB = 1024 * 1024
NT_DIMS = (((1,), (1,)), ((), ()))
TN_DIMS = (((0,), (0,)), ((), ()))


class _Group(NamedTuple):
    n_seq: int
    seq_rows: int
    valid_rows: int

    @property
    def rows(self):
        return self.n_seq * self.seq_rows

    @property
    def dense(self):
        return self.valid_rows == self.seq_rows


SAMPLE_SEQ_ROWS = 8


def _params(n_grid, vmem_mib):
    return pltpu.CompilerParams(dimension_semantics=("arbitrary",) * n_grid, vmem_limit_bytes=vmem_mib * MIB)


def _mod_spec(grp, tm, comp, tn=D, with_j=False):
    per_comp = D // tn
    if grp.seq_rows >= tm:
        per_seq = grp.seq_rows // tm
        if with_j:
            return pl.BlockSpec((None, 1, tn), lambda i, j: (i // per_seq, 0, comp * per_comp + j))
        return pl.BlockSpec((None, 1, tn), lambda i, *_: (i // per_seq, 0, comp * per_comp))
    if with_j:
        return pl.BlockSpec((None, tm, tn), lambda i, j: (0, i, comp * per_comp + j))
    return pl.BlockSpec((None, tm, tn), lambda i, *_: (0, i, comp * per_comp))


def _rms_modulate(x, g, sc, sh):
    xn = x * lax.rsqrt(jnp.mean(x * x, axis=-1, keepdims=True) + EPS)
    return xn * g * (1.0 + sc) + sh


def _silu(x):
    return x * jax.nn.sigmoid(x)


def _ada_body(c_ref, w_ref, b_ref, o_ref):
    a = _silu(c_ref[...]).astype(BF16)
    o_ref[...] = jnp.dot(a, w_ref[...].astype(BF16), preferred_element_type=F32) + b_ref[...]


def _ada_mod(c_all, w_ada, b_ada):
    rows, tn = c_all.shape[0], 1024
    return pl.pallas_call(
        _ada_body,
        grid=(6 * D // tn,),
        in_specs=[
            pl.BlockSpec((rows, D), lambda j: (0, 0)),
            pl.BlockSpec((D, tn), lambda j: (0, j)),
            pl.BlockSpec((1, tn), lambda j: (0, j)),
        ],
        out_specs=pl.BlockSpec((rows, tn), lambda j: (0, j)),
        out_shape=jax.ShapeDtypeStruct((rows, 6 * D), F32),
        compiler_params=_params(1, 40),
        name="ada_mod",
    )(c_all, w_ada, b_ada)


def _norm_mm_body(x_ref, g_ref, sc_ref, sh_ref, w_ref, o_ref, h_scr):
    @pl.when(pl.program_id(1) == 0)
    def _():
        h_scr[...] = _rms_modulate(x_ref[...], g_ref[...], sc_ref[...], sh_ref[...]).astype(BF16)

    o_ref[...] = jnp.dot(h_scr[...], w_ref[...], preferred_element_type=F32)


def _norm_matmul(grp, x, g, mod, w_cat):
    tm = 1024 if grp.dense else 256
    tn = 1024
    n = w_cat.shape[1]
    return pl.pallas_call(
        _norm_mm_body,
        grid=(grp.rows // tm, n // tn),
        in_specs=[
            pl.BlockSpec((tm, D), lambda i, j: (i, 0)),
            pl.BlockSpec((1, D), lambda i, j: (0, 0)),
            _mod_spec(grp, tm, 1),
            _mod_spec(grp, tm, 0),
            pl.BlockSpec((D, tn), lambda i, j: (0, j)),
        ],
        out_specs=pl.BlockSpec((tm, tn), lambda i, j: (i, j)),
        out_shape=jax.ShapeDtypeStruct((grp.rows, n), F32),
        scratch_shapes=[pltpu.VMEM((tm, D), BF16)],
        compiler_params=_params(2, 48),
        name="norm_proj",
    )(x, g, mod, mod, w_cat)


def _split3(x):
    hi = x.astype(BF16).astype(F32)
    r1 = x - hi
    mid = r1.astype(BF16).astype(F32)
    lo = (r1 - mid).astype(BF16).astype(F32)
    return hi, mid, lo


def _gla_body(qk_ref, v_ref, gr_ref, glr_ref, w2_ref, ba_ref, gn_ref, s0_ref, og_ref, so_ref, s_scr,
              *, chunk, sub, valid, n_chunks):
    c = chunk
    n = pl.program_id(1)

    @pl.when(n == 0)
    def _():
        s_scr[...] = s0_ref[...]

    row = lax.broadcasted_iota(jnp.int32, (c, 1), 0)
    live = row < valid
    tri = (lax.broadcasted_iota(jnp.int32, (c, c), 1) <= lax.broadcasted_iota(jnp.int32, (c, c), 0)).astype(BF16)
    lane = lax.broadcasted_iota(jnp.int32, (sub, c), 1)
    rowi = lax.broadcasted_iota(jnp.int32, (sub, c), 0)
    glr = glr_ref[:, 0:GATE_RANK].astype(BF16)
    ones8 = jnp.ones((8, GLA_DV), BF16)

    for hh in range(GLA_H):
        q = qk_ref[:, COL_GQ + hh * GLA_DK:COL_GQ + (hh + 1) * GLA_DK] * (GLA_DK ** -0.5)
        k = qk_ref[:, COL_GK + hh * GLA_DK:COL_GK + (hh + 1) * GLA_DK]
        v = v_ref[:, hh * GLA_DV:(hh + 1) * GLA_DV]
        z = jnp.dot(glr, w2_ref[:, hh * GLA_DK:(hh + 1) * GLA_DK], preferred_element_type=F32)
        z = z + ba_ref[:, hh * GLA_DK:(hh + 1) * GLA_DK]
        g = (jnp.minimum(z, 0.0) - jnp.log1p(jnp.exp(-jnp.abs(z)))) * (1.0 / GATE_TEMP)
        if valid < c:
            g = jnp.where(live, g, 0.0)
            k = jnp.where(live, k, 0.0)
        b = sum(jnp.dot(tri, p.astype(BF16), preferred_element_type=F32) for p in _split3(g))
        vb = v.astype(BF16)

        o_intra = jnp.zeros((c, GLA_DV), F32)
        p_rows = []
        for i in range(c // sub):
            r0 = i * sub
            b_i, q_i, k_i = b[r0:r0 + sub], q[r0:r0 + sub], k[r0:r0 + sub]
            att = jnp.zeros((sub, c), F32)
            for s in range(min(sub, valid - r0)):
                e = jnp.exp(jnp.minimum(b_i - b_i[s:s + 1], 0.0))
                col = jnp.sum(q_i * k_i[s:s + 1] * e, axis=-1, keepdims=True)
                if c == sub:
                    o_intra = o_intra + jnp.where(row >= s, col, 0.0) * v[s:s + 1]
                else:
                    att = jnp.where(lane == r0 + s, col, att)
            if c == sub:
                continue
            att = jnp.where(lane <= rowi + r0, att, 0.0)
            if i > 0:
                ref = b[r0 - 1:r0]
                qt = (q_i * jnp.exp(b_i - ref)).astype(BF16)
                kt = (k * jnp.exp(jnp.minimum(ref - b, 0.0))).astype(BF16)
                off = lax.dot_general(qt, kt, NT_DIMS, preferred_element_type=F32)
                att = jnp.where(lane < r0, off, att)
            p_rows.append(att)
        if c != sub:
            p_mat = jnp.concatenate(p_rows, axis=0).astype(BF16)
            o_intra = jnp.dot(p_mat, vb, preferred_element_type=F32)

        s_old = s_scr[hh]
        o = jnp.dot((q * jnp.exp(b)).astype(BF16), s_old.astype(BF16), preferred_element_type=F32) + o_intra
        on = o * lax.rsqrt(jnp.mean(o * o, axis=-1, keepdims=True) + EPS) * gn_ref[...]
        gr = gr_ref[:, hh * GLA_DV:(hh + 1) * GLA_DV]
        og_ref[:, hh * GLA_DV:(hh + 1) * GLA_DV] = (on * _silu(gr)).astype(og_ref.dtype)

        b_end = b[c - 1:c]
        khat = (k * jnp.exp(b_end - b)).astype(BF16)
        d_s = lax.dot_general(khat, vb, TN_DIMS, preferred_element_type=F32)
        pieces = jnp.concatenate(list(_split3(jnp.exp(b_end))) + [jnp.zeros((5, GLA_DK), F32)], axis=0)
        decay = lax.dot_general(pieces.astype(BF16), ones8, TN_DIMS, preferred_element_type=F32)
        s_scr[hh] = s_old * decay + d_s

    @pl.when(n == n_chunks - 1)
    def _():
        so_ref[...] = s_scr[...]


def _gla(grp, proj, w2, ba, gn, s0, out_dtype):
    chunk, sub = (64, 16) if grp.dense else (grp.seq_rows, grp.seq_rows)
    nc = grp.seq_rows // chunk
    body = functools.partial(_gla_body, chunk=chunk, sub=sub, valid=min(chunk, grp.valid_rows), n_chunks=nc)
    state_spec = pl.BlockSpec((None, GLA_H, GLA_DK, GLA_DV), lambda b, n: (b, 0, 0, 0))
    return pl.pallas_call(
        body,
        grid=(grp.n_seq, nc),
        in_specs=[
            pl.BlockSpec((chunk, 2048), lambda b, n: (b * nc + n, 0)),
            pl.BlockSpec((chunk, 2048), lambda b, n: (b * nc + n, COL_GV // 2048)),
            pl.BlockSpec((chunk, 2048), lambda b, n: (b * nc + n, COL_GR // 2048)),
            pl.BlockSpec((chunk, GLR_BLOCK), lambda b, n: (b * nc + n, COL_GLR // GLR_BLOCK)),
            pl.BlockSpec((GATE_RANK, GLA_H * GLA_DK), lambda b, n: (0, 0)),
            pl.BlockSpec((1, GLA_H * GLA_DK), lambda b, n: (0, 0)),
            pl.BlockSpec((1, GLA_DV), lambda b, n: (0, 0)),
            state_spec,
        ],
        out_specs=[pl.BlockSpec((chunk, 2048), lambda b, n: (b * nc + n, 0)), state_spec],
        out_shape=[
            jax.ShapeDtypeStruct((grp.rows, 2048), out_dtype),
            jax.ShapeDtypeStruct((grp.n_seq, GLA_H, GLA_DK, GLA_DV), F32),
        ],
        scratch_shapes=[pltpu.VMEM((GLA_H, GLA_DK, GLA_DV), F32)],
        compiler_params=_params(2, 40),
        name="gla",
    )(proj, proj, proj, proj, w2, ba, gn, s0)


def _swa_heads(q_ref, k_of, v_of, sinks_ref, o_ref, mask, rows):
    for kv in range(SWA_KV):
        k2, v2 = k_of(kv), v_of(kv)
        heads = [kv * SWA_G + g for g in range(SWA_G)]
        qs = jnp.concatenate([q_ref[:, h * SWA_HD:(h + 1) * SWA_HD] for h in heads], axis=0).astype(BF16)
        s = lax.dot_general(qs, k2, NT_DIMS, preferred_element_type=F32) * (SWA_HD ** -0.5)
        s = jnp.where(mask, s, -jnp.inf)
        sink = jnp.concatenate([jnp.full((rows, 1), sinks_ref[h], F32) for h in heads], axis=0)
        m = jnp.maximum(jnp.max(s, axis=-1, keepdims=True), sink)
        p = jnp.exp(s - m)
        den = jnp.sum(p, axis=-1, keepdims=True) + jnp.exp(sink - m)
        o = jnp.dot((p / den).astype(BF16), v2, preferred_element_type=F32)
        for g, h in enumerate(heads):
            o_ref[:, h * SWA_HD:(h + 1) * SWA_HD] = o[g * rows:(g + 1) * rows].astype(o_ref.dtype)


def _swa_prompt_body(sinks_ref, q_ref, kp_ref, kc_ref, vp_ref, vc_ref, o_ref):
    i = pl.program_id(1)
    w = WINDOW
    shape = (SWA_G * w, 2 * w)
    qpos = (lax.broadcasted_iota(jnp.int32, shape, 0) & (w - 1)) + w
    kpos = lax.broadcasted_iota(jnp.int32, shape, 1)
    dist = qpos - kpos
    mask = (dist >= 0) & (dist < w) & ((kpos >= w) | (i > 0))

    def two_blocks(prev_ref, cur_ref):
        return lambda kv: jnp.concatenate(
            [prev_ref[:, kv * SWA_HD:(kv + 1) * SWA_HD], cur_ref[:, kv * SWA_HD:(kv + 1) * SWA_HD]], axis=0
        ).astype(BF16)

    _swa_heads(q_ref, two_blocks(kp_ref, kc_ref), two_blocks(vp_ref, vc_ref), sinks_ref, o_ref, mask, w)


def _swa_prompt(grp, proj, sinks):
    w = WINDOW
    nb = grp.seq_rows // w
    kvw = SWA_KV * SWA_HD
    cur = lambda col: (lambda b, i: (b * nb + i, col))
    prev = lambda col: (lambda b, i: (b * nb + jnp.maximum(i - 1, 0), col))
    return pl.pallas_call(
        _swa_prompt_body,
        grid=(grp.n_seq, nb),
        in_specs=[
            pl.BlockSpec(memory_space=pltpu.SMEM),
            pl.BlockSpec((w, 2048), cur(COL_SQ // 2048)),
            pl.BlockSpec((w, kvw), prev(COL_SK // kvw)),
            pl.BlockSpec((w, kvw), cur(COL_SK // kvw)),
            pl.BlockSpec((w, kvw), prev(COL_SV // kvw)),
            pl.BlockSpec((w, kvw), cur(COL_SV // kvw)),
        ],
        out_specs=pl.BlockSpec((w, 2048), lambda b, i: (b * nb + i, 0)),
        out_shape=jax.ShapeDtypeStruct((grp.rows, 2048), BF16),
        compiler_params=_params(2, 40),
        name="swa_prompt",
    )(sinks, proj, proj, proj, proj, proj)


def _swa_sample_body(sinks_ref, q_ref, kn_ref, vn_ref, ck_ref, cv_ref, o_ref, *, rows, valid):
    w = WINDOW
    shape = (SWA_G * rows, 2 * w)
    qi = lax.broadcasted_iota(jnp.int32, shape, 0) & (rows - 1)
    col = lax.broadcasted_iota(jnp.int32, shape, 1)
    mask = ((col < w) & (col > qi)) | ((col >= w) & (col - w <= qi) & (col < w + valid))
    pad = jnp.zeros((w - rows, SWA_HD), F32)

    def cache_then_new(cache_ref, new_ref):
        return lambda kv: jnp.concatenate(
            [cache_ref[:, kv * SWA_HD:(kv + 1) * SWA_HD], new_ref[:, kv * SWA_HD:(kv + 1) * SWA_HD], pad], axis=0
        ).astype(BF16)

    _swa_heads(q_ref, cache_then_new(ck_ref, kn_ref), cache_then_new(cv_ref, vn_ref), sinks_ref, o_ref, mask, rows)


def _swa_sample(grp, proj, cache_k, cache_v, sinks):
    w = WINDOW
    rows = grp.seq_rows
    kvw = SWA_KV * SWA_HD
    body = functools.partial(_swa_sample_body, rows=rows, valid=grp.valid_rows)
    cache_spec = pl.BlockSpec((None, w, kvw), lambda b: (b, 0, 0))
    return pl.pallas_call(
        body,
        grid=(grp.n_seq,),
        in_specs=[
            pl.BlockSpec(memory_space=pltpu.SMEM),
            pl.BlockSpec((rows, 2048), lambda b: (b, COL_SQ // 2048)),
            pl.BlockSpec((rows, kvw), lambda b: (b, COL_SK // kvw)),
            pl.BlockSpec((rows, kvw), lambda b: (b, COL_SV // kvw)),
            cache_spec,
            cache_spec,
        ],
        out_specs=pl.BlockSpec((rows, 2048), lambda b: (b, 0)),
        out_shape=jax.ShapeDtypeStruct((grp.rows, 2048), F32),
        compiler_params=_params(1, 32),
        name="swa_sample",
    )(sinks, proj, proj, proj, cache_k, cache_v)


def _merge_body(og_ref, os_ref, ga_ref, gb_ref, x_ref, gt_ref, w_ref, o_ref, m_scr):
    @pl.when(pl.program_id(1) == 0)
    def _():
        merged = (jax.nn.sigmoid(ga_ref[...]) * og_ref[...].astype(F32)
                  + jax.nn.sigmoid(gb_ref[...]) * os_ref[...].astype(F32))
        m_scr[...] = merged.astype(BF16)

    y = jnp.dot(m_scr[...], w_ref[...], preferred_element_type=F32)
    o_ref[...] = x_ref[...] + gt_ref[...] * y


def _merge_out(grp, og, os_, proj, x, mod, w_out):
    tm = 512 if grp.dense else 256
    tn = 1024
    return pl.pallas_call(
        _merge_body,
        grid=(grp.rows // tm, D // tn),
        in_specs=[
            pl.BlockSpec((tm, D), lambda i, j: (i, 0)),
            pl.BlockSpec((tm, D), lambda i, j: (i, 0)),
            pl.BlockSpec((tm, D), lambda i, j: (i, COL_GA // D)),
            pl.BlockSpec((tm, D), lambda i, j: (i, COL_GB // D)),
            pl.BlockSpec((tm, tn), lambda i, j: (i, j)),
            _mod_spec(grp, tm, 2, tn=tn, with_j=True),
            pl.BlockSpec((D, tn), lambda i, j: (0, j)),
        ],
        out_specs=pl.BlockSpec((tm, tn), lambda i, j: (i, j)),
        out_shape=jax.ShapeDtypeStruct((grp.rows, D), F32),
        scratch_shapes=[pltpu.VMEM((tm, D), BF16)],
        compiler_params=_params(2, 52),
        name="merge_out",
    )(og, os_, proj, proj, x, mod, w_out)


ROUTE_TM = 256
ROUTE_LANES = 128

_SLABS = (("a", 0, 16), ("a", 1, 8), ("b", 0, 16), ("b", 1, 8), ("a", 2, 8), ("a", 3, 8), ("a", 4, 8))


def _candidate_tables():
    flat, a_idx, ok, seen = [], [], [], set()
    for kind, fixed, n in _SLABS:
        for r in range(n):
            a, b = (fixed, r) if kind == "a" else (r, fixed)
            good = (a + 1) * (b + 1) <= TOPK and (a, b) not in seen
            if good:
                seen.add((a, b))
            flat.append(a * TOPK + b if good else 1e9)
            a_idx.append(a if good else -1)
            ok.append(1.0 if good else 0.0)
    assert len(seen) == 50
    tile = lambda v: np.tile(np.asarray(v, np.float32)[:, None], (1, ROUTE_LANES))
    return tile(flat), tile(a_idx), tile(ok)


N_CAND = sum(n for _, _, n in _SLABS)


def _take_top(vals, tie_key, n_take, on_take):
    for t in range(n_take):
        m = jnp.max(vals, axis=0, keepdims=True)
        first = jnp.min(jnp.where(vals == m, tie_key, 2e9), axis=0, keepdims=True)
        taken = tie_key == first
        on_take(t, m, taken)
        vals = jnp.where(taken, -jnp.inf, vals)


def _route_body(x_ref, g_ref, sc_ref, sh_ref, wq_ref, keys_ref, flat_ref, aidx_ref, ok_ref,
                h_ref, r2_ref, e2_ref, n1_ref, e1_ref, st_scr, rk_scr, tv_scr):
    hb = _rms_modulate(x_ref[...], g_ref[...], sc_ref[...], sh_ref[...]).astype(BF16)
    h_ref[...] = hb
    qp = jnp.dot(hb, wq_ref[...], preferred_element_type=F32)
    n_sub = ROUTE_TM // ROUTE_LANES
    for hc in range(2 * PEER_H):
        st = lax.dot_general(keys_ref[hc], qp[:, hc * N_KEYS:(hc + 1) * N_KEYS].astype(BF16), NT_DIMS,
                             preferred_element_type=F32)
        for u in range(n_sub):
            st_scr[hc, u] = st[:, u * ROUTE_LANES:(u + 1) * ROUTE_LANES]

    key_id = lax.broadcasted_iota(jnp.int32, (N_KEYS, ROUTE_LANES), 0).astype(F32)

    def side_topk(hc, carry):
        for u in range(n_sub):
            state = {"rank": jnp.full((N_KEYS, ROUTE_LANES), 99.0, F32), "tops": []}

            def on_take(t, m, taken):
                state["rank"] = jnp.where(taken, float(t), state["rank"])
                state["tops"].append(m)

            _take_top(st_scr[hc, u], key_id, TOPK, on_take)
            rk_scr[hc, u] = state["rank"]
            tv_scr[hc, u] = jnp.concatenate(state["tops"], axis=0)
        return carry

    lax.fori_loop(0, 2 * PEER_H, side_topk, 0)

    flat, aidx, ok = flat_ref[...], aidx_ref[...], ok_ref[...] > 0.0

    def joint_topk(h, carry):
        for u in range(n_sub):
            t1, t2 = tv_scr[2 * h, u], tv_scr[2 * h + 1, u]
            slabs = []
            for kind, fixed, n in _SLABS:
                slabs.append(t1[fixed:fixed + 1] + t2[0:n] if kind == "a" else t1[0:n] + t2[fixed:fixed + 1])
            cand = jnp.where(ok, jnp.concatenate(slabs, axis=0), -jnp.inf)
            state = {"sel": jnp.zeros((N_CAND, ROUTE_LANES), F32)}

            def on_take(t, m, taken):
                state["sel"] = jnp.where(taken, 1.0, state["sel"])

            _take_top(cand, flat, TOPK, on_take)
            sel = state["sel"]
            z = jnp.sum(jnp.where(sel > 0.0, jnp.exp(cand - cand[0:1]), 0.0), axis=0, keepdims=True)
            rank1, rank2 = rk_scr[2 * h, u], rk_scr[2 * h + 1, u]
            n1 = jnp.zeros((N_KEYS, ROUTE_LANES), F32)
            for a in range(TOPK):
                n_a = jnp.sum(jnp.where(aidx == float(a), sel, 0.0), axis=0, keepdims=True)
                n1 = jnp.where(rank1 == float(a), n_a, n1)
            lanes = slice(u * ROUTE_LANES, (u + 1) * ROUTE_LANES)
            r2_ref[h, 0, :, lanes] = rank2
            n1_ref[h, 0, :, lanes] = n1
            e1_ref[h, 0, :, lanes] = jnp.exp(st_scr[2 * h, u] - t1[0:1]) / z
            e2_ref[h, 0, :, lanes] = jnp.exp(st_scr[2 * h + 1, u] - t2[0:1])
        return carry

    lax.fori_loop(0, PEER_H, joint_topk, 0)


def _route(grp, x, g, mod, wq, keys, tables):
    tm = ROUTE_TM
    n_sub = tm // ROUTE_LANES
    n_tiles = grp.rows // tm
    table_spec = pl.BlockSpec((N_CAND, ROUTE_LANES), lambda i: (0, 0))
    route_spec = pl.BlockSpec((PEER_H, 1, N_KEYS, tm), lambda i: (0, i, 0, 0))
    route_shape = jax.ShapeDtypeStruct((PEER_H, n_tiles, N_KEYS, tm), F32)
    return pl.pallas_call(
        _route_body,
        grid=(n_tiles,),
        in_specs=[
            pl.BlockSpec((tm, D), lambda i: (i, 0)),
            pl.BlockSpec((1, D), lambda i: (0, 0)),
            _mod_spec(grp, tm, 4),
            _mod_spec(grp, tm, 3),
            pl.BlockSpec((D, D), lambda i: (0, 0)),
            pl.BlockSpec((2 * PEER_H, N_KEYS, N_KEYS), lambda i: (0, 0, 0)),
            table_spec, table_spec, table_spec,
        ],
        out_specs=[pl.BlockSpec((tm, D), lambda i: (i, 0)), route_spec, route_spec, route_spec, route_spec],
        out_shape=[jax.ShapeDtypeStruct((grp.rows, D), BF16), route_shape, route_shape, route_shape, route_shape],
        scratch_shapes=[
            pltpu.VMEM((2 * PEER_H, n_sub, N_KEYS, ROUTE_LANES), F32),
            pltpu.VMEM((2 * PEER_H, n_sub, N_KEYS, ROUTE_LANES), F32),
            pltpu.VMEM((2 * PEER_H, n_sub, TOPK, ROUTE_LANES), F32),
        ],
        compiler_params=_params(1, 48),
        name="peer_route",
    )(x, g, mod, mod, wq, keys, *tables)


PEER_TT = 512
PEER_CH = ROUTE_TM
PEER_ET = 512


def _peer_body(h_ref, u_ref, vt_ref, r2_ref, e2_ref, n1_ref, e1_ref, x_ref, gt_ref, fg_ref, o_ref, acc_scr,
               *, final_norm):
    j = pl.program_id(1)
    n_ch = PEER_TT // PEER_CH
    n_blk = PEER_ET // N_KEYS

    @pl.when(j == 0)
    def _():
        acc_scr[...] = jnp.zeros_like(acc_scr)

    u, vt = u_ref[...], vt_ref[...]
    for ch in range(n_ch):
        act = lax.dot_general(u, h_ref[ch], NT_DIMS, preferred_element_type=F32)
        gelu = 0.5 * act * (1.0 + lax.erf(act * (2.0 ** -0.5)))
        blocks = []
        for ib in range(n_blk):
            i1 = j * n_blk + ib
            gate = jnp.zeros((N_KEYS, PEER_CH), F32)
            for h in range(PEER_H):
                keep = r2_ref[h, ch] < n1_ref[h, ch, pl.ds(i1, 1), :]
                gate = gate + jnp.where(keep, e2_ref[h, ch] * e1_ref[h, ch, pl.ds(i1, 1), :], 0.0)
            blocks.append((gate * gelu[ib * N_KEYS:(ib + 1) * N_KEYS]).astype(BF16))
        p = jnp.concatenate(blocks, axis=0)
        acc_scr[ch] += jnp.dot(vt, p, preferred_element_type=F32)

    @pl.when(j == pl.num_programs(1) - 1)
    def _():
        for ch in range(n_ch):
            rows = slice(ch * PEER_CH, (ch + 1) * PEER_CH)
            gt = gt_ref[...] if gt_ref.shape[0] == 1 else gt_ref[rows, :]
            x2 = x_ref[rows, :] + gt * acc_scr[ch].T
            if final_norm:
                x2 = x2 * lax.rsqrt(jnp.mean(x2 * x2, axis=-1, keepdims=True) + EPS) * fg_ref[...]
            o_ref[rows, :] = x2


def _peer(grp, h2, routing, u, vt, x, mod, final_g, final_norm):
    tt, et = PEER_TT, PEER_ET
    n_ch = tt // PEER_CH
    once = pl.Buffered(1)
    route_spec = pl.BlockSpec((PEER_H, n_ch, N_KEYS, PEER_CH), lambda i, j: (0, i, 0, 0), pipeline_mode=once)
    body = functools.partial(_peer_body, final_norm=final_norm)
    return pl.pallas_call(
        body,
        grid=(grp.rows // tt, N_EXPERTS // et),
        in_specs=[
            pl.BlockSpec((n_ch, PEER_CH, D), lambda i, j: (i, 0, 0)),
            pl.BlockSpec((et, D), lambda i, j: (j, 0)),
            pl.BlockSpec((D, et), lambda i, j: (0, j)),
            route_spec, route_spec, route_spec, route_spec,
            pl.BlockSpec((tt, D), lambda i, j: (i, 0), pipeline_mode=once),
            _mod_spec(grp, tt, 5),
            pl.BlockSpec((1, D), lambda i, j: (0, 0)),
        ],
        out_specs=pl.BlockSpec((tt, D), lambda i, j: (i, 0)),
        out_shape=jax.ShapeDtypeStruct((grp.rows, D), F32),
        scratch_shapes=[pltpu.VMEM((n_ch, D, PEER_CH), F32)],
        compiler_params=_params(2, 56),
        name="peer_dense",
    )(h2.reshape(grp.rows // PEER_CH, PEER_CH, D), u, vt, *routing, x, mod, final_g)


def _proj_weights(w_in):
    gla_end, glr_end = 6144, 6160
    sq, sk, sv, ga, gb = (w_in[:, a:b] for a, b in ((6160, 8208), (8208, 8464), (8464, 8720), (8720, 10768),
                                                     (10768, 12816)))
    pad = jnp.zeros((D, N_PROJ - COL_GLR - GATE_RANK), w_in.dtype)
    return jnp.concatenate([w_in[:, :gla_end], sq, ga, gb, sk, sv, w_in[:, gla_end:glr_end], pad],
                           axis=1).astype(BF16)


def kernel(x_prompt, x_sample, state_gla, cache_swa_k, cache_swa_v, c_prompt, c_sample, w_ada, b_ada, norm1_g,
           norm2_g, w_in, w_alpha2, b_alpha, gla_norm_g, swa_sinks, w_out, peer_wq, peer_keys, peer_u, peer_v,
           final_g):
    n_p, l_p = x_prompt.shape[0], x_prompt.shape[1]
    n_s, l_s = x_sample.shape[0], x_sample.shape[1]
    prompt = _Group(n_p, l_p, l_p)
    sample = _Group(n_s, SAMPLE_SEQ_ROWS, l_s)
    kvw = SWA_KV * SWA_HD
    xs = {
        prompt: x_prompt.reshape(prompt.rows, D),
        sample: jnp.pad(x_sample, ((0, 0), (0, sample.seq_rows - l_s), (0, 0))).reshape(sample.rows, D),
    }
    c_all = jnp.concatenate([c_prompt, c_sample], axis=0)
    tables = tuple(jnp.asarray(t) for t in _candidate_tables())
    fg = final_g.reshape(1, D)
    zero_state = jnp.zeros((n_p, GLA_H, GLA_DK, GLA_DV), F32)
    outs = {prompt: ([], [], []), sample: ([], [], [])}

    for l in range(DEPTH):
        mod = _ada_mod(c_all, w_ada[l], b_ada[l].reshape(1, 6 * D))
        mods = {
            prompt: mod[:n_p].reshape(n_p, 1, 6 * D),
            sample: jnp.repeat(mod[n_p:], sample.seq_rows, axis=0).reshape(1, sample.rows, 6 * D),
        }
        w_cat = _proj_weights(w_in[l])
        w2 = w_alpha2[l].astype(BF16)
        ba = b_alpha[l].reshape(1, GLA_H * GLA_DK)
        gn = gla_norm_g[l].reshape(1, GLA_DV)
        w_o = w_out[l].astype(BF16)
        wq = peer_wq[l].astype(BF16)
        keys = peer_keys[l].reshape(2 * PEER_H, N_KEYS, N_KEYS).astype(BF16)
        u = peer_u[l].astype(BF16)
        vt = peer_v[l].astype(BF16).T
        g1, g2 = norm1_g[l].reshape(1, D), norm2_g[l].reshape(1, D)

        for grp in (prompt, sample):
            x, m = xs[grp], mods[grp]
            proj = _norm_matmul(grp, x, g1, m, w_cat)
            heads = lambda a: a.reshape(grp.n_seq, grp.seq_rows, SWA_KV, SWA_HD)
            k_new, v_new = heads(proj[:, COL_SK:COL_SK + kvw]), heads(proj[:, COL_SV:COL_SV + kvw])
            if grp.dense:
                og, s_new = _gla(grp, proj, w2, ba, gn, zero_state, BF16)
                os_ = _swa_prompt(grp, proj, swa_sinks[l])
                k_buf, v_buf = k_new[:, -WINDOW:], v_new[:, -WINDOW:]
            else:
                og, s_new = _gla(grp, proj, w2, ba, gn, state_gla[l], F32)
                ck = cache_swa_k[l].reshape(grp.n_seq, WINDOW, kvw)
                cv = cache_swa_v[l].reshape(grp.n_seq, WINDOW, kvw)
                os_ = _swa_sample(grp, proj, ck, cv, swa_sinks[l])
                k_buf = jnp.concatenate([cache_swa_k[l][:, l_s:], k_new[:, :l_s]], axis=1)
                v_buf = jnp.concatenate([cache_swa_v[l][:, l_s:], v_new[:, :l_s]], axis=1)
            x1 = _merge_out(grp, og, os_, proj, x, m, w_o)
            h2, *routing = _route(grp, x1, g2, m, wq, keys, tables)
            xs[grp] = _peer(grp, h2, routing, u, vt, x1, m, fg, final_norm=(l == DEPTH - 1))
            for acc, val in zip(outs[grp], (s_new, k_buf, v_buf)):
                acc.append(val)

    y_prompt = xs[prompt].reshape(x_prompt.shape)
    y_sample = xs[sample].reshape(n_s, sample.seq_rows, D)[:, :l_s]
    stacked = lambda grp: tuple(jnp.stack(a) for a in outs[grp])
    return (y_prompt, y_sample) + stacked(prompt) + stacked(sample)
```

```python
import functools
from typing import NamedTuple

import numpy as np
import jax
import jax.numpy as jnp
from jax import lax
from jax.experimental import pallas as pl
from jax.experimental.pallas import tpu as pltpu

F32 = jnp.float32
BF16 = jnp.bfloat16

D = 2048
DEPTH = 2
GLA_H, GLA_DK, GLA_DV, GATE_RANK, GATE_TEMP = 4, 256, 512, 16, 16.0
SWA_HD, SWA_KV, SWA_G, WINDOW = 64, 4, 8, 128
PEER_H, N_KEYS, TOPK = 8, 128, 16
N_EXPERTS = N_KEYS * N_KEYS
EPS = 1e-6

COL_GQ, COL_GK, COL_GV, COL_GR = 0, 1024, 2048, 4096
COL_SQ, COL_GA, COL_GB = 6144, 8192, 10240
COL_SK, COL_SV, COL_GLR = 12288, 12544, 12800
N_PROJ = 13312
GLR_BLOCK = 128

MIB = 1024 * 1024
NT_DIMS = (((1,), (1,)), ((), ()))
TN_DIMS = (((0,), (0,)), ((), ()))


class _Group(NamedTuple):
    n_seq: int
    seq_rows: int
    valid_rows: int

    @property
    def rows(self):
        return self.n_seq * self.seq_rows

    @property
    def dense(self):
        return self.valid_rows == self.seq_rows


SAMPLE_SEQ_ROWS = 8


def _params(n_grid, vmem_mib):
    return pltpu.CompilerParams(dimension_semantics=("arbitrary",) * n_grid, vmem_limit_bytes=vmem_mib * MIB)


def _mod_spec(grp, tm, comp, tn=D, with_j=False):
    per_comp = D // tn
    if grp.seq_rows >= tm:
        per_seq = grp.seq_rows // tm
        if with_j:
            return pl.BlockSpec((None, 1, tn), lambda i, j: (i // per_seq, 0, comp * per_comp + j))
        return pl.BlockSpec((None, 1, tn), lambda i, *_: (i // per_seq, 0, comp * per_comp))
    if with_j:
        return pl.BlockSpec((None, tm, tn), lambda i, j: (0, i, comp * per_comp + j))
    return pl.BlockSpec((None, tm, tn), lambda i, *_: (0, i, comp * per_comp))


def _layer_spec(layer, block, index_map):
    return pl.BlockSpec((None,) + tuple(block), lambda *grid: (layer,) + tuple(index_map(*grid)))


def _rms_modulate(x, g, sc, sh):
    xn = x * lax.rsqrt(jnp.mean(x * x, axis=-1, keepdims=True) + EPS)
    return xn * g * (1.0 + sc) + sh


def _silu(x):
    return x * jax.nn.sigmoid(x)


def _ada_body(c_ref, w_ref, b_ref, o_ref):
    a = _silu(c_ref[...]).astype(BF16)
    o_ref[...] = jnp.dot(a, w_ref[...].astype(BF16), preferred_element_type=F32) + b_ref[...]


def _ada_mod(layer, c_all, w_ada, b_ada):
    rows, tn = c_all.shape[0], 1024
    return pl.pallas_call(
        _ada_body,
        grid=(6 * D // tn,),
        in_specs=[
            pl.BlockSpec((rows, D), lambda j: (0, 0)),
            _layer_spec(layer, (D, tn), lambda j: (0, j)),
            _layer_spec(layer, (1, tn), lambda j: (0, j)),
        ],
        out_specs=pl.BlockSpec((rows, tn), lambda j: (0, j)),
        out_shape=jax.ShapeDtypeStruct((rows, 6 * D), F32),
        compiler_params=_params(1, 40),
        name="ada_mod",
    )(c_all, w_ada, b_ada)


def _norm_mm_body(x_ref, g_ref, sc_ref, sh_ref, w_ref, o_ref, h_scr):
    @pl.when(pl.program_id(1) == 0)
    def _():
        h_scr[...] = _rms_modulate(x_ref[...], g_ref[...], sc_ref[...], sh_ref[...]).astype(BF16)

    o_ref[...] = jnp.dot(h_scr[...], w_ref[...], preferred_element_type=F32)


def _norm_matmul(layer, grp, x, g, mod, w_cat):
    tm = 1024 if grp.dense else 256
    tn = 1024
    n = w_cat.shape[-1]
    return pl.pallas_call(
        _norm_mm_body,
        grid=(grp.rows // tm, n // tn),
        in_specs=[
            pl.BlockSpec((tm, D), lambda i, j: (i, 0)),
            _layer_spec(layer, (1, D), lambda i, j: (0, 0)),
            _mod_spec(grp, tm, 1),
            _mod_spec(grp, tm, 0),
            _layer_spec(layer, (D, tn), lambda i, j: (0, j)),
        ],
        out_specs=pl.BlockSpec((tm, tn), lambda i, j: (i, j)),
        out_shape=jax.ShapeDtypeStruct((grp.rows, n), F32),
        scratch_shapes=[pltpu.VMEM((tm, D), BF16)],
        compiler_params=_params(2, 48),
        name="norm_proj",
    )(x, g, mod, mod, w_cat)


def _split3(x):
    hi = x.astype(BF16).astype(F32)
    r1 = x - hi
    mid = r1.astype(BF16).astype(F32)
    lo = (r1 - mid).astype(BF16).astype(F32)
    return hi, mid, lo


def _gla_body(qk_ref, v_ref, gr_ref, glr_ref, w2_ref, ba_ref, gn_ref, *rest, chunk, sub, valid, n_chunks, has_state,
              chained):
    s0_ref = rest[0] if has_state else None
    og_ref, so_ref, s_scr = rest[int(has_state) + int(chained):]
    c = chunk
    n = pl.program_id(1)

    @pl.when(n == 0)
    def _():
        s_scr[...] = s0_ref[...] if has_state else jnp.zeros_like(s_scr)

    row = lax.broadcasted_iota(jnp.int32, (c, 1), 0)
    live = row < valid
    tri = (lax.broadcasted_iota(jnp.int32, (c, c), 1) <= lax.broadcasted_iota(jnp.int32, (c, c), 0)).astype(BF16)
    lane = lax.broadcasted_iota(jnp.int32, (sub, c), 1)
    rowi = lax.broadcasted_iota(jnp.int32, (sub, c), 0)
    glr = glr_ref[:, 0:GATE_RANK].astype(BF16)
    ones8 = jnp.ones((8, GLA_DV), BF16)

    for hh in range(GLA_H):
        q = qk_ref[:, COL_GQ + hh * GLA_DK:COL_GQ + (hh + 1) * GLA_DK] * (GLA_DK ** -0.5)
        k = qk_ref[:, COL_GK + hh * GLA_DK:COL_GK + (hh + 1) * GLA_DK]
        v = v_ref[:, hh * GLA_DV:(hh + 1) * GLA_DV]
        z = jnp.dot(glr, w2_ref[:, hh * GLA_DK:(hh + 1) * GLA_DK], preferred_element_type=F32)
        z = z + ba_ref[:, hh * GLA_DK:(hh + 1) * GLA_DK]
        g = (jnp.minimum(z, 0.0) - jnp.log(1.0 + jnp.exp(-jnp.abs(z)))) * (1.0 / GATE_TEMP)
        if valid < c:
            g = jnp.where(live, g, 0.0)
            k = jnp.where(live, k, 0.0)
        b = sum(jnp.dot(tri, p.astype(BF16), preferred_element_type=F32) for p in _split3(g))
        vb = v.astype(BF16)

        o_intra = jnp.zeros((c, GLA_DV), F32)
        p_rows = []
        for i in range(c // sub):
            r0 = i * sub
            b_i, q_i, k_i = b[r0:r0 + sub], q[r0:r0 + sub], k[r0:r0 + sub]
            att = jnp.zeros((sub, c), F32)
            for s in range(min(sub, valid - r0)):
                e = jnp.exp(jnp.minimum(b_i - b_i[s:s + 1], 0.0))
                col = jnp.sum(q_i * k_i[s:s + 1] * e, axis=-1, keepdims=True)
                if c == sub:
                    o_intra = o_intra + jnp.where(row >= s, col, 0.0) * v[s:s + 1]
                else:
                    att = jnp.where(lane == r0 + s, col, att)
            if c == sub:
                continue
            att = jnp.where(lane <= rowi + r0, att, 0.0)
            if i > 0:
                ref = b[r0 - 1:r0]
                qt = (q_i * jnp.exp(b_i - ref)).astype(BF16)
                kt = (k * jnp.exp(jnp.minimum(ref - b, 0.0))).astype(BF16)
                off = lax.dot_general(qt, kt, NT_DIMS, preferred_element_type=F32)
                att = jnp.where(lane < r0, off, att)
            p_rows.append(att)
        if c != sub:
            p_mat = jnp.concatenate(p_rows, axis=0).astype(BF16)
            o_intra = jnp.dot(p_mat, vb, preferred_element_type=F32)

        s_old = s_scr[hh]
        o = jnp.dot((q * jnp.exp(b)).astype(BF16), s_old.astype(BF16), preferred_element_type=F32) + o_intra
        on = o * lax.rsqrt(jnp.mean(o * o, axis=-1, keepdims=True) + EPS) * gn_ref[...]
        gr = gr_ref[:, hh * GLA_DV:(hh + 1) * GLA_DV]
        og_ref[:, hh * GLA_DV:(hh + 1) * GLA_DV] = (on * _silu(gr)).astype(og_ref.dtype)

        b_end = b[c - 1:c]
        khat = (k * jnp.exp(b_end - b)).astype(BF16)
        d_s = lax.dot_general(khat, vb, TN_DIMS, preferred_element_type=F32)
        pieces = jnp.concatenate(list(_split3(jnp.exp(b_end))) + [jnp.zeros((5, GLA_DK), F32)], axis=0)
        decay = lax.dot_general(pieces.astype(BF16), ones8, TN_DIMS, preferred_element_type=F32)
        s_scr[hh] = s_old * decay + d_s

    @pl.when(n == n_chunks - 1)
    def _():
        so_ref[...] = s_scr[...]


def _gla(layer, grp, proj, w2, ba, gn, s0, states, out_dtype):
    chunk, sub = (64, 8) if grp.dense else (grp.seq_rows, grp.seq_rows)
    nc = grp.seq_rows // chunk
    has_state, chained = s0 is not None, states is not None
    body = functools.partial(_gla_body, chunk=chunk, sub=sub, valid=min(chunk, grp.valid_rows), n_chunks=nc,
                             has_state=has_state, chained=chained)
    state_spec = _layer_spec(layer, (None, GLA_H, GLA_DK, GLA_DV), lambda b, n: (b, 0, 0, 0))
    in_specs = [
        pl.BlockSpec((chunk, 2048), lambda b, n: (b * nc + n, 0)),
        pl.BlockSpec((chunk, 2048), lambda b, n: (b * nc + n, COL_GV // 2048)),
        pl.BlockSpec((chunk, 2048), lambda b, n: (b * nc + n, COL_GR // 2048)),
        pl.BlockSpec((chunk, GLR_BLOCK), lambda b, n: (b * nc + n, COL_GLR // GLR_BLOCK)),
        _layer_spec(layer, (GATE_RANK, GLA_H * GLA_DK), lambda b, n: (0, 0)),
        _layer_spec(layer, (1, GLA_H * GLA_DK), lambda b, n: (0, 0)),
        _layer_spec(layer, (1, GLA_DV), lambda b, n: (0, 0)),
    ]
    args = [proj, proj, proj, proj, w2, ba, gn]
    if has_state:
        in_specs.append(state_spec)
        args.append(s0)
    if chained:
        in_specs.append(pl.BlockSpec(memory_space=pl.ANY))
        args.append(states)
    return pl.pallas_call(
        body,
        grid=(grp.n_seq, nc),
        in_specs=in_specs,
        out_specs=[pl.BlockSpec((chunk, 2048), lambda b, n: (b * nc + n, 0)), state_spec],
        out_shape=[
            jax.ShapeDtypeStruct((grp.rows, 2048), out_dtype),
            jax.ShapeDtypeStruct((DEPTH, grp.n_seq, GLA_H, GLA_DK, GLA_DV), F32),
        ],
        input_output_aliases={len(args) - 1: 1} if chained else {},
        scratch_shapes=[pltpu.VMEM((GLA_H, GLA_DK, GLA_DV), F32)],
        compiler_params=_params(2, 40),
        name="gla",
    )(*args)


def _sink_attention(q, k_scaled, v, bias, sink):
    st = lax.dot_general(k_scaled, q, NT_DIMS, preferred_element_type=F32) + bias
    m = jnp.maximum(jnp.max(st, axis=0, keepdims=True), sink)
    p = jnp.exp(st - m)
    den = jnp.sum(p, axis=0, keepdims=True) + jnp.exp(sink - m)
    ot = lax.dot_general(v, p.astype(BF16), TN_DIMS, preferred_element_type=F32)
    return (ot / den).T


def _kv_head(parts, kv, scale=None):
    cols = slice(kv * SWA_HD, (kv + 1) * SWA_HD)
    x = jnp.concatenate([r if r.shape[1] == SWA_HD else r[:, cols] for r in parts], axis=0)
    return (x if scale is None else x * scale).astype(BF16)


def _swa_prompt_body(sinks_ref, q_ref, kp_ref, kc_ref, vp_ref, vc_ref, o_ref, *, layer):
    i = pl.program_id(1)
    w = WINDOW
    kpos = lax.broadcasted_iota(jnp.int32, (2 * w, w), 0)
    dist = lax.broadcasted_iota(jnp.int32, (2 * w, w), 1) + w - kpos
    bias = jnp.where((dist >= 0) & (dist < w) & ((kpos >= w) | (i > 0)), 0.0, -jnp.inf)
    for kv in range(SWA_KV):
        k2 = _kv_head([kp_ref, kc_ref], kv, SWA_HD ** -0.5)
        v2 = _kv_head([vp_ref, vc_ref], kv)
        for h in range(kv * SWA_G, (kv + 1) * SWA_G):
            cols = slice(h * SWA_HD, (h + 1) * SWA_HD)
            o = _sink_attention(q_ref[:, cols].astype(BF16), k2, v2, bias, sinks_ref[layer, h])
            o_ref[:, cols] = o.astype(o_ref.dtype)


def _swa_prompt(layer, grp, proj, sinks):
    w = WINDOW
    nb = grp.seq_rows // w
    kvw = SWA_KV * SWA_HD
    cur = lambda col: (lambda b, i: (b * nb + i, col))
    prev = lambda col: (lambda b, i: (b * nb + jnp.maximum(i - 1, 0), col))
    return pl.pallas_call(
        functools.partial(_swa_prompt_body, layer=layer),
        grid=(grp.n_seq, nb),
        in_specs=[
            pl.BlockSpec(memory_space=pltpu.SMEM),
            pl.BlockSpec((w, 2048), cur(COL_SQ // 2048)),
            pl.BlockSpec((w, kvw), prev(COL_SK // kvw)),
            pl.BlockSpec((w, kvw), cur(COL_SK // kvw)),
            pl.BlockSpec((w, kvw), prev(COL_SV // kvw)),
            pl.BlockSpec((w, kvw), cur(COL_SV // kvw)),
        ],
        out_specs=pl.BlockSpec((w, 2048), lambda b, i: (b * nb + i, 0)),
        out_shape=jax.ShapeDtypeStruct((grp.rows, 2048), BF16),
        compiler_params=_params(2, 40),
        name="swa_prompt",
    )(sinks, proj, proj, proj, proj, proj)


def _swa_sample_body(sinks_ref, q_ref, kn_ref, vn_ref, ck_ref, cv_ref, o_ref, *, layer, rows, valid):
    w = WINDOW
    shape = (2 * w, SWA_G * rows)
    qi = lax.broadcasted_iota(jnp.int32, shape, 1) & (rows - 1)
    key = lax.broadcasted_iota(jnp.int32, shape, 0)
    visible = ((key < w) & (key > qi)) | ((key >= w) & (key - w <= qi) & (key < w + valid))
    bias = jnp.where(visible, 0.0, -jnp.inf)
    pad = jnp.zeros((w - rows, SWA_HD), F32)
    for kv in range(SWA_KV):
        heads = range(kv * SWA_G, (kv + 1) * SWA_G)
        k2 = _kv_head([ck_ref, kn_ref, pad], kv, SWA_HD ** -0.5)
        v2 = _kv_head([cv_ref, vn_ref, pad], kv)
        q = jnp.concatenate([q_ref[:, h * SWA_HD:(h + 1) * SWA_HD] for h in heads], axis=0).astype(BF16)
        sink = jnp.concatenate([jnp.full((1, rows), sinks_ref[layer, h], F32) for h in heads], axis=1)
        o = _sink_attention(q, k2, v2, bias, sink)
        for g, h in enumerate(heads):
            o_ref[:, h * SWA_HD:(h + 1) * SWA_HD] = o[g * rows:(g + 1) * rows].astype(o_ref.dtype)


def _swa_sample(layer, grp, proj, cache_k, cache_v, sinks):
    w = WINDOW
    rows = grp.seq_rows
    kvw = SWA_KV * SWA_HD
    body = functools.partial(_swa_sample_body, layer=layer, rows=rows, valid=grp.valid_rows)
    cache_spec = _layer_spec(layer, (None, w, kvw), lambda b: (b, 0, 0))
    return pl.pallas_call(
        body,
        grid=(grp.n_seq,),
        in_specs=[
            pl.BlockSpec(memory_space=pltpu.SMEM),
            pl.BlockSpec((rows, 2048), lambda b: (b, COL_SQ // 2048)),
            pl.BlockSpec((rows, kvw), lambda b: (b, COL_SK // kvw)),
            pl.BlockSpec((rows, kvw), lambda b: (b, COL_SV // kvw)),
            cache_spec,
            cache_spec,
        ],
        out_specs=pl.BlockSpec((rows, 2048), lambda b: (b, 0)),
        out_shape=jax.ShapeDtypeStruct((grp.rows, 2048), F32),
        compiler_params=_params(1, 32),
        name="swa_sample",
    )(sinks, proj, proj, proj, cache_k, cache_v)


def _merge_body(og_ref, os_ref, ga_ref, gb_ref, x_ref, gt_ref, w_ref, o_ref, m_scr):
    @pl.when(pl.program_id(1) == 0)
    def _():
        merged = (jax.nn.sigmoid(ga_ref[...]) * og_ref[...].astype(F32)
                  + jax.nn.sigmoid(gb_ref[...]) * os_ref[...].astype(F32))
        m_scr[...] = merged.astype(BF16)

    y = jnp.dot(m_scr[...], w_ref[...], preferred_element_type=F32)
    o_ref[...] = x_ref[...] + gt_ref[...] * y


def _merge_out(layer, grp, og, os_, proj, x, mod, w_out):
    tm = 512 if grp.dense else 256
    tn = 1024
    return pl.pallas_call(
        _merge_body,
        grid=(grp.rows // tm, D // tn),
        in_specs=[
            pl.BlockSpec((tm, D), lambda i, j: (i, 0)),
            pl.BlockSpec((tm, D), lambda i, j: (i, 0)),
            pl.BlockSpec((tm, D), lambda i, j: (i, COL_GA // D)),
            pl.BlockSpec((tm, D), lambda i, j: (i, COL_GB // D)),
            pl.BlockSpec((tm, tn), lambda i, j: (i, j)),
            _mod_spec(grp, tm, 2, tn=tn, with_j=True),
            _layer_spec(layer, (D, tn), lambda i, j: (0, j)),
        ],
        out_specs=pl.BlockSpec((tm, tn), lambda i, j: (i, j)),
        out_shape=jax.ShapeDtypeStruct((grp.rows, D), F32),
        scratch_shapes=[pltpu.VMEM((tm, D), BF16)],
        compiler_params=_params(2, 52),
        name="merge_out",
    )(og, os_, proj, proj, x, mod, w_out)


ROUTE_TM = 256
ROUTE_LANES = 128

_SLABS = (("a", 0, 16), ("a", 1, 8), ("b", 0, 16), ("b", 1, 8), ("a", 2, 8), ("a", 3, 8), ("a", 4, 8))


def _candidate_tables():
    flat, a_idx, ok, seen = [], [], [], set()
    for kind, fixed, n in _SLABS:
        for r in range(n):
            a, b = (fixed, r) if kind == "a" else (r, fixed)
            good = (a + 1) * (b + 1) <= TOPK and (a, b) not in seen
            if good:
                seen.add((a, b))
            flat.append(a * TOPK + b if good else 1e9)
            a_idx.append(a if good else -1)
            ok.append(1.0 if good else 0.0)
    assert len(seen) == 50
    tile = lambda v: np.tile(np.asarray(v, np.float32)[:, None], (1, ROUTE_LANES))
    return tile(flat), tile(a_idx), tile(ok)


N_CAND = sum(n for _, _, n in _SLABS)


def _take_top(vals, tie_key, n_take, on_take, exact):
    for t in range(n_take):
        m = jnp.max(vals, axis=0, keepdims=True)
        taken = vals == m
        if exact:
            first = jnp.min(jnp.where(taken, tie_key, 2e9), axis=0, keepdims=True)
            taken = tie_key == first
        on_take(t, m, taken)
        vals = jnp.where(taken, -jnp.inf, vals)


def _tied(taken_count):
    return jnp.max(taken_count) > float(TOPK)


def _route_body(x_ref, g_ref, sc_ref, sh_ref, wq_ref, keys_ref, flat_ref, aidx_ref, ok_ref,
                h_ref, r2_ref, e2_ref, n1_ref, e1_ref, st_scr, rk_scr, tv_scr, sel_scr):
    h = _rms_modulate(x_ref[...], g_ref[...], sc_ref[...], sh_ref[...])
    h_ref[...] = h.T.astype(BF16)
    qp = jnp.dot(h.astype(BF16), wq_ref[...], preferred_element_type=F32)
    n_sub = ROUTE_TM // ROUTE_LANES
    for hc in range(2 * PEER_H):
        st = lax.dot_general(keys_ref[hc], qp[:, hc * N_KEYS:(hc + 1) * N_KEYS].astype(BF16), NT_DIMS,
                             preferred_element_type=F32)
        for u in range(n_sub):
            st_scr[hc, u] = st[:, u * ROUTE_LANES:(u + 1) * ROUTE_LANES]

    key_id = lax.broadcasted_iota(jnp.int32, (N_KEYS, ROUTE_LANES), 0).astype(F32)

    def side_ranks(hc, u, exact):
        state = {"rank": jnp.full((N_KEYS, ROUTE_LANES), 99.0, F32), "tops": []}

        def on_take(t, m, taken):
            state["rank"] = jnp.where(taken, float(t), state["rank"])
            state["tops"].append(m)

        _take_top(st_scr[hc, u], key_id, TOPK, on_take, exact)
        rk_scr[hc, u] = state["rank"]
        tv_scr[hc, u] = jnp.concatenate(state["tops"], axis=0)
        return jnp.sum(jnp.where(state["rank"] < 99.0, 1.0, 0.0), axis=0, keepdims=True)

    def side_topk(hc, carry):
        counts = [side_ranks(hc, u, exact=False) for u in range(n_sub)]

        @pl.when(_tied(jnp.concatenate(counts, axis=0)))
        def _():
            for u in range(n_sub):
                side_ranks(hc, u, exact=True)
        return carry

    lax.fori_loop(0, 2 * PEER_H, side_topk, 0)

    flat, aidx, ok = flat_ref[...], aidx_ref[...], ok_ref[...] > 0.0

    def joint_topk(h, carry):
        def candidates(u):
            t1, t2 = tv_scr[2 * h, u], tv_scr[2 * h + 1, u]
            slabs = []
            for kind, fixed, n in _SLABS:
                slabs.append(t1[fixed:fixed + 1] + t2[0:n] if kind == "a" else t1[0:n] + t2[fixed:fixed + 1])
            return jnp.where(ok, jnp.concatenate(slabs, axis=0), -jnp.inf)

        def select(u, exact):
            state = {"sel": jnp.zeros((N_CAND, ROUTE_LANES), F32)}

            def on_take(t, m, taken):
                state["sel"] = jnp.where(taken, 1.0, state["sel"])

            _take_top(candidates(u), flat, TOPK, on_take, exact)
            sel_scr[u] = state["sel"]
            return jnp.sum(state["sel"], axis=0, keepdims=True)

        counts = [select(u, exact=False) for u in range(n_sub)]

        @pl.when(_tied(jnp.concatenate(counts, axis=0)))
        def _():
            for u in range(n_sub):
                select(u, exact=True)

        for u in range(n_sub):
            t1, t2 = tv_scr[2 * h, u], tv_scr[2 * h + 1, u]
            cand, sel = candidates(u), sel_scr[u]
            z = jnp.sum(jnp.where(sel > 0.0, jnp.exp(cand - cand[0:1]), 0.0), axis=0, keepdims=True)
            rank1, rank2 = rk_scr[2 * h, u], rk_scr[2 * h + 1, u]
            n1 = jnp.zeros((N_KEYS, ROUTE_LANES), F32)
            for a in range(TOPK):
                n_a = jnp.sum(jnp.where(aidx == float(a), sel, 0.0), axis=0, keepdims=True)
                n1 = jnp.where(rank1 == float(a), n_a, n1)
            lanes = slice(u * ROUTE_LANES, (u + 1) * ROUTE_LANES)
            r2_ref[h, 0, :, lanes] = rank2.astype(r2_ref.dtype)
            n1_ref[h, 0, :, lanes] = n1
            e1_ref[h, 0, :, lanes] = jnp.exp(st_scr[2 * h, u] - t1[0:1]) / z
            e2_ref[h, 0, :, lanes] = jnp.exp(st_scr[2 * h + 1, u] - t2[0:1]).astype(e2_ref.dtype)
        return carry

    lax.fori_loop(0, PEER_H, joint_topk, 0)


def _route(layer, grp, x, g, mod, wq, keys, tables):
    tm = ROUTE_TM
    n_sub = tm // ROUTE_LANES
    n_tiles = grp.rows // tm
    table_spec = pl.BlockSpec((N_CAND, ROUTE_LANES), lambda i: (0, 0))
    route_spec = pl.BlockSpec((PEER_H, 1, N_KEYS, tm), lambda i: (0, i, 0, 0))
    tile_shape = jax.ShapeDtypeStruct((PEER_H, n_tiles, N_KEYS, tm), BF16)
    row_shape = jax.ShapeDtypeStruct((PEER_H, n_tiles, N_KEYS, tm), F32)
    return pl.pallas_call(
        _route_body,
        grid=(n_tiles,),
        in_specs=[
            pl.BlockSpec((tm, D), lambda i: (i, 0)),
            _layer_spec(layer, (1, D), lambda i: (0, 0)),
            _mod_spec(grp, tm, 4),
            _mod_spec(grp, tm, 3),
            _layer_spec(layer, (D, D), lambda i: (0, 0)),
            _layer_spec(layer, (2 * PEER_H, N_KEYS, N_KEYS), lambda i: (0, 0, 0)),
            table_spec, table_spec, table_spec,
        ],
        out_specs=[pl.BlockSpec((D, tm), lambda i: (0, i)), route_spec, route_spec, route_spec, route_spec],
        out_shape=[jax.ShapeDtypeStruct((D, grp.rows), BF16), tile_shape, tile_shape, row_shape, row_shape],
        scratch_shapes=[
            pltpu.VMEM((2 * PEER_H, n_sub, N_KEYS, ROUTE_LANES), F32),
            pltpu.VMEM((2 * PEER_H, n_sub, N_KEYS, ROUTE_LANES), F32),
            pltpu.VMEM((2 * PEER_H, n_sub, TOPK, ROUTE_LANES), F32),
            pltpu.VMEM((n_sub, N_CAND, ROUTE_LANES), F32),
        ],
        compiler_params=_params(1, 48),
        name="peer_route",
    )(x, g, mod, mod, wq, keys, *tables)


PEER_TT = 1024
PEER_CH = ROUTE_TM
PEER_ET = 512


def _peer_gates(r2_ref, e2_ref, n1_ref, e1_ref, g_scr, slot, step):
    n_blk = PEER_ET // N_KEYS
    for ch in range(PEER_TT // PEER_CH):
        for ib in range(n_blk):
            i1 = step * n_blk + ib
            gate = jnp.zeros((N_KEYS, PEER_CH), BF16)
            for h in range(PEER_H):
                keep = r2_ref[h, ch] < n1_ref[h, ch, pl.ds(i1, 1), :].astype(BF16)
                weight = e2_ref[h, ch] * e1_ref[h, ch, pl.ds(i1, 1), :].astype(BF16)
                gate = gate + jnp.where(keep, weight, jnp.zeros_like(weight))
            g_scr[slot, ch, ib * N_KEYS:(ib + 1) * N_KEYS, :] = gate


def _peer_body(h_ref, u_ref, v_ref, r2_ref, e2_ref, n1_ref, e1_ref, o_ref, g_scr):
    j = pl.program_id(1)
    last = pl.num_programs(1) - 1
    slot = j % 2
    tables = (r2_ref, e2_ref, n1_ref, e1_ref)

    @pl.when(j == 0)
    def _():
        o_ref[...] = jnp.zeros_like(o_ref)
        _peer_gates(*tables, g_scr, 0, 0)

    _peer_gates(*tables, g_scr, 1 - slot, jnp.minimum(j + 1, last))

    u, v = u_ref[...], v_ref[...]
    for ch in range(PEER_TT // PEER_CH):
        cols = slice(ch * PEER_CH, (ch + 1) * PEER_CH)
        hc = h_ref[:, cols]
        half_e, half_d = PEER_ET // 2, D // 2
        ps = []
        for s in range(2):
            experts = slice(s * half_e, (s + 1) * half_e)
            act = jnp.dot(u[experts], hc, preferred_element_type=F32)
            gelu = 0.5 * act * (1.0 + lax.erf(act * (2.0 ** -0.5)))
            ps.append(g_scr[slot, ch, experts, :] * gelu.astype(BF16))
        p = jnp.concatenate(ps, axis=0)
        for s in range(2):
            rows = slice(s * half_d, (s + 1) * half_d)
            o_ref[rows, cols] += lax.dot_general(v[:, rows], p, TN_DIMS, preferred_element_type=F32)


def _peer(layer, grp, h2t, routing, u, v):
    tt, et = PEER_TT, PEER_ET
    n_ch = tt // PEER_CH
    once = pl.Buffered(1)
    route_spec = pl.BlockSpec((PEER_H, n_ch, N_KEYS, PEER_CH), lambda i, j: (0, i, 0, 0), pipeline_mode=once)
    return pl.pallas_call(
        _peer_body,
        grid=(grp.rows // tt, N_EXPERTS // et),
        in_specs=[
            pl.BlockSpec((D, tt), lambda i, j: (0, i), pipeline_mode=once),
            _layer_spec(layer, (et, D), lambda i, j: (j, 0)),
            _layer_spec(layer, (et, D), lambda i, j: (j, 0)),
            route_spec, route_spec, route_spec, route_spec,
        ],
        out_specs=pl.BlockSpec((D, tt), lambda i, j: (0, i)),
        out_shape=jax.ShapeDtypeStruct((D, grp.rows), F32),
        scratch_shapes=[pltpu.VMEM((2, n_ch, et, PEER_CH), BF16)],
        compiler_params=_params(2, 56),
        name="peer_dense",
    )(h2t, u, v, *routing)


def _peer_residual_body(f_ref, x_ref, gt_ref, fg_ref, o_ref, *, final_norm):
    x2 = x_ref[...] + gt_ref[...] * f_ref[...].T
    if final_norm:
        x2 = x2 * lax.rsqrt(jnp.mean(x2 * x2, axis=-1, keepdims=True) + EPS) * fg_ref[...]
    o_ref[...] = x2


def _peer_residual(grp, ffn_t, x, mod, final_g, final_norm):
    tm = 256
    return pl.pallas_call(
        functools.partial(_peer_residual_body, final_norm=final_norm),
        grid=(grp.rows // tm,),
        in_specs=[
            pl.BlockSpec((D, tm), lambda i: (0, i)),
            pl.BlockSpec((tm, D), lambda i: (i, 0)),
            _mod_spec(grp, tm, 5),
            pl.BlockSpec((1, D), lambda i: (0, 0)),
        ],
        out_specs=pl.BlockSpec((tm, D), lambda i: (i, 0)),
        out_shape=jax.ShapeDtypeStruct((grp.rows, D), F32),
        compiler_params=_params(1, 32),
        name="peer_residual",
    )(ffn_t, x, mod, final_g)


def _proj_weights(w_in):
    gla_end, glr_end = 6144, 6160
    sq, sk, sv, ga, gb = (w_in[..., a:b] for a, b in ((6160, 8208), (8208, 8464), (8464, 8720), (8720, 10768),
                                                       (10768, 12816)))
    pad = jnp.zeros(w_in.shape[:-1] + (N_PROJ - COL_GLR - GATE_RANK,), w_in.dtype)
    return jnp.concatenate([w_in[..., :gla_end], sq, ga, gb, sk, sv, w_in[..., gla_end:glr_end], pad],
                           axis=-1).astype(BF16)


def kernel(x_prompt, x_sample, state_gla, cache_swa_k, cache_swa_v, c_prompt, c_sample, w_ada, b_ada, norm1_g,
           norm2_g, w_in, w_alpha2, b_alpha, gla_norm_g, swa_sinks, w_out, peer_wq, peer_keys, peer_u, peer_v,
           final_g):
    n_p, l_p = x_prompt.shape[0], x_prompt.shape[1]
    n_s, l_s = x_sample.shape[0], x_sample.shape[1]
    prompt = _Group(n_p, l_p, l_p)
    sample = _Group(n_s, SAMPLE_SEQ_ROWS, l_s)
    kvw = SWA_KV * SWA_HD
    xs = {
        prompt: x_prompt.reshape(prompt.rows, D),
        sample: jnp.pad(x_sample, ((0, 0), (0, sample.seq_rows - l_s), (0, 0))).reshape(sample.rows, D),
    }
    c_all = jnp.concatenate([c_prompt, c_sample], axis=0)
    tables = tuple(jnp.asarray(t) for t in _candidate_tables())
    fg = final_g.reshape(1, D)

    row = lambda a: a.reshape(DEPTH, 1, -1)
    b_ada_r, g1, g2, ba, gn = row(b_ada), row(norm1_g), row(norm2_g), row(b_alpha), row(gla_norm_g)
    w_cat = _proj_weights(w_in)
    w2, w_o, wq = w_alpha2.astype(BF16), w_out.astype(BF16), peer_wq.astype(BF16)
    keys = peer_keys.reshape(DEPTH, 2 * PEER_H, N_KEYS, N_KEYS).astype(BF16)
    u, v = peer_u.astype(BF16), peer_v.astype(BF16)
    cache_k = cache_swa_k.reshape(DEPTH, n_s, WINDOW, kvw)
    cache_v = cache_swa_v.reshape(DEPTH, n_s, WINDOW, kvw)

    states = {prompt: None, sample: None}
    k_bufs, v_bufs = {prompt: [], sample: []}, {prompt: [], sample: []}
    for l in range(DEPTH):
        mod = _ada_mod(l, c_all, w_ada, b_ada_r)
        mods = {
            prompt: mod[:n_p].reshape(n_p, 1, 6 * D),
            sample: jnp.repeat(mod[n_p:], sample.seq_rows, axis=0).reshape(1, sample.rows, 6 * D),
        }
        for grp in (prompt, sample):
            x, m = xs[grp], mods[grp]
            proj = _norm_matmul(l, grp, x, g1, m, w_cat)
            heads = lambda a: a.reshape(grp.n_seq, grp.seq_rows, SWA_KV, SWA_HD)
            k_new, v_new = heads(proj[:, COL_SK:COL_SK + kvw]), heads(proj[:, COL_SV:COL_SV + kvw])
            if grp.dense:
                og, states[grp] = _gla(l, grp, proj, w2, ba, gn, None, states[grp], BF16)
                os_ = _swa_prompt(l, grp, proj, swa_sinks)
                k_buf, v_buf = k_new[:, -WINDOW:], v_new[:, -WINDOW:]
            else:
                og, states[grp] = _gla(l, grp, proj, w2, ba, gn, state_gla, states[grp], F32)
                os_ = _swa_sample(l, grp, proj, cache_k, cache_v, swa_sinks)
                k_buf = jnp.concatenate([cache_swa_k[l][:, l_s:], k_new[:, :l_s]], axis=1)
                v_buf = jnp.concatenate([cache_swa_v[l][:, l_s:], v_new[:, :l_s]], axis=1)
            x1 = _merge_out(l, grp, og, os_, proj, x, m, w_o)
            h2t, *routing = _route(l, grp, x1, g2, m, wq, keys, tables)
            ffn_t = _peer(l, grp, h2t, routing, u, v)
            xs[grp] = _peer_residual(grp, ffn_t, x1, m, fg, final_norm=(l == DEPTH - 1))
            k_bufs[grp].append(k_buf)
            v_bufs[grp].append(v_buf)

    y_prompt = xs[prompt].reshape(x_prompt.shape)
    y_sample = xs[sample].reshape(n_s, sample.seq_rows, D)[:, :l_s]
    caches = lambda grp: (states[grp], jnp.stack(k_bufs[grp]), jnp.stack(v_bufs[grp]))
    return (y_prompt, y_sample) + caches(prompt) + caches(sample)
```

```python
import functools
from typing import NamedTuple

import numpy as np
import jax
import jax.numpy as jnp
from jax import lax
from jax.experimental import pallas as pl
from jax.experimental.pallas import tpu as pltpu

F32 = jnp.float32
BF16 = jnp.bfloat16

D = 2048
DEPTH = 2
GLA_H, GLA_DK, GLA_DV, GATE_RANK, GATE_TEMP = 4, 256, 512, 16, 16.0
SWA_HD, SWA_KV, SWA_G, WINDOW = 64, 4, 8, 128
PEER_H, N_KEYS, TOPK = 8, 128, 16
N_EXPERTS = N_KEYS * N_KEYS
EPS = 1e-6

COL_GQ, COL_GK, COL_GV, COL_GR = 0, 1024, 2048, 4096
COL_SQ, COL_GA, COL_GB = 6144, 8192, 10240
COL_SK, COL_SV, COL_GLR = 12288, 12544, 12800
N_PROJ = 13312
GLR_BLOCK = 128

MIB = 1024 * 1024
NT_DIMS = (((1,), (1,)), ((), ()))
TN_DIMS = (((0,), (0,)), ((), ()))


class _Group(NamedTuple):
    n_seq: int
    seq_rows: int
    valid_rows: int

    @property
    def rows(self):
        return self.n_seq * self.seq_rows

    @property
    def dense(self):
        return self.valid_rows == self.seq_rows


SAMPLE_SEQ_ROWS = 8


def _params(n_grid, vmem_mib):
    return pltpu.CompilerParams(dimension_semantics=("arbitrary",) * n_grid, vmem_limit_bytes=vmem_mib * MIB)


def _mod_spec(grp, tm, comp, tn=D, with_j=False):
    per_comp = D // tn
    if grp.seq_rows >= tm:
        per_seq = grp.seq_rows // tm
        if with_j:
            return pl.BlockSpec((None, 1, tn), lambda i, j: (i // per_seq, 0, comp * per_comp + j))
        return pl.BlockSpec((None, 1, tn), lambda i, *_: (i // per_seq, 0, comp * per_comp))
    if with_j:
        return pl.BlockSpec((None, tm, tn), lambda i, j: (0, i, comp * per_comp + j))
    return pl.BlockSpec((None, tm, tn), lambda i, *_: (0, i, comp * per_comp))


def _layer_spec(layer, block, index_map):
    return pl.BlockSpec((None,) + tuple(block), lambda *grid: (layer,) + tuple(index_map(*grid)))


def _rms_modulate(x, g, sc, sh):
    xn = x * lax.rsqrt(jnp.mean(x * x, axis=-1, keepdims=True) + EPS)
    return xn * g * (1.0 + sc) + sh


def _silu(x):
    return x * jax.nn.sigmoid(x)


def _ada_body(c_ref, w_ref, b_ref, o_ref):
    a = _silu(c_ref[...]).astype(BF16)
    o_ref[...] = jnp.dot(a, w_ref[...].astype(BF16), preferred_element_type=F32) + b_ref[...]


def _ada_mod(layer, c_all, w_ada, b_ada):
    rows, tn = c_all.shape[0], 1024
    return pl.pallas_call(
        _ada_body,
        grid=(6 * D // tn,),
        in_specs=[
            pl.BlockSpec((rows, D), lambda j: (0, 0)),
            _layer_spec(layer, (D, tn), lambda j: (0, j)),
            _layer_spec(layer, (1, tn), lambda j: (0, j)),
        ],
        out_specs=pl.BlockSpec((rows, tn), lambda j: (0, j)),
        out_shape=jax.ShapeDtypeStruct((rows, 6 * D), F32),
        compiler_params=_params(1, 40),
        name="ada_mod",
    )(c_all, w_ada, b_ada)


def _norm_mm_body(x_ref, g_ref, sc_ref, sh_ref, w_ref, o_ref, h_scr):
    @pl.when(pl.program_id(1) == 0)
    def _():
        h_scr[...] = _rms_modulate(x_ref[...], g_ref[...], sc_ref[...], sh_ref[...]).astype(BF16)

    o_ref[...] = jnp.dot(h_scr[...], w_ref[...], preferred_element_type=F32)


def _norm_matmul(layer, grp, x, g, mod, w_cat):
    tm = 1024 if grp.dense else 256
    tn = 1024
    n = w_cat.shape[-1]
    return pl.pallas_call(
        _norm_mm_body,
        grid=(grp.rows // tm, n // tn),
        in_specs=[
            pl.BlockSpec((tm, D), lambda i, j: (i, 0)),
            _layer_spec(layer, (1, D), lambda i, j: (0, 0)),
            _mod_spec(grp, tm, 1),
            _mod_spec(grp, tm, 0),
            _layer_spec(layer, (D, tn), lambda i, j: (0, j)),
        ],
        out_specs=pl.BlockSpec((tm, tn), lambda i, j: (i, j)),
        out_shape=jax.ShapeDtypeStruct((grp.rows, n), F32),
        scratch_shapes=[pltpu.VMEM((tm, D), BF16)],
        compiler_params=_params(2, 48),
        name="norm_proj",
    )(x, g, mod, mod, w_cat)


def _split3(x):
    hi = x.astype(BF16).astype(F32)
    r1 = x - hi
    mid = r1.astype(BF16).astype(F32)
    lo = (r1 - mid).astype(BF16).astype(F32)
    return hi, mid, lo


def _gla_body(qk_ref, v_ref, gr_ref, glr_ref, w2_ref, ba_ref, gn_ref, *rest, chunk, sub, valid, n_chunks, has_state,
              chained):
    s0_ref = rest[0] if has_state else None
    og_ref, so_ref, s_scr = rest[int(has_state) + int(chained):]
    c = chunk
    n = pl.program_id(1)

    @pl.when(n == 0)
    def _():
        s_scr[...] = s0_ref[...] if has_state else jnp.zeros_like(s_scr)

    row = lax.broadcasted_iota(jnp.int32, (c, 1), 0)
    live = row < valid
    tri = (lax.broadcasted_iota(jnp.int32, (c, c), 1) <= lax.broadcasted_iota(jnp.int32, (c, c), 0)).astype(BF16)
    lane = lax.broadcasted_iota(jnp.int32, (sub, c), 1)
    rowi = lax.broadcasted_iota(jnp.int32, (sub, c), 0)
    glr = glr_ref[:, 0:GATE_RANK].astype(BF16)
    ones8 = jnp.ones((8, GLA_DV), BF16)

    heads = range(GLA_H)
    dk = lambda hh: slice(hh * GLA_DK, (hh + 1) * GLA_DK)
    dv = lambda hh: slice(hh * GLA_DV, (hh + 1) * GLA_DV)
    qs = [qk_ref[:, COL_GQ + hh * GLA_DK:COL_GQ + (hh + 1) * GLA_DK] * (GLA_DK ** -0.5) for hh in heads]
    ks = [qk_ref[:, COL_GK + hh * GLA_DK:COL_GK + (hh + 1) * GLA_DK] for hh in heads]
    vs = [v_ref[:, dv(hh)] for hh in heads]
    zs = [jnp.dot(glr, w2_ref[:, dk(hh)], preferred_element_type=F32) + ba_ref[:, dk(hh)] for hh in heads]
    gs = [(jnp.minimum(z, 0.0) - jnp.log(1.0 + jnp.exp(-jnp.abs(z)))) * (1.0 / GATE_TEMP) for z in zs]
    if valid < c:
        gs = [jnp.where(live, g, 0.0) for g in gs]
        ks = [jnp.where(live, k, 0.0) for k in ks]
    bs = [sum(jnp.dot(tri, p.astype(BF16), preferred_element_type=F32) for p in _split3(g)) for g in gs]
    vbs = [v.astype(BF16) for v in vs]

    o_intras = [jnp.zeros((c, GLA_DV), F32) for _ in heads]
    p_rows = [[] for _ in heads]
    for i in range(c // sub):
        r0 = i * sub
        for hh in heads:
            b, q, k, v = bs[hh], qs[hh], ks[hh], vs[hh]
            b_i, q_i, k_i = b[r0:r0 + sub], q[r0:r0 + sub], k[r0:r0 + sub]
            att = jnp.zeros((sub, c), F32)
            for s in range(min(sub, valid - r0)):
                e = jnp.exp(jnp.minimum(b_i - b_i[s:s + 1], 0.0))
                col = jnp.sum(q_i * k_i[s:s + 1] * e, axis=-1, keepdims=True)
                if c == sub:
                    o_intras[hh] = o_intras[hh] + jnp.where(row >= s, col, 0.0) * v[s:s + 1]
                else:
                    att = jnp.where(lane == r0 + s, col, att)
            if c == sub:
                continue
            att = jnp.where(lane <= rowi + r0, att, 0.0)
            if i > 0:
                ref = b[r0 - 1:r0]
                qt = (q_i * jnp.exp(b_i - ref)).astype(BF16)
                kt = (k * jnp.exp(jnp.minimum(ref - b, 0.0))).astype(BF16)
                off = lax.dot_general(qt, kt, NT_DIMS, preferred_element_type=F32)
                att = jnp.where(lane < r0, off, att)
            p_rows[hh].append(att)
    if c != sub:
        o_intras = [jnp.dot(jnp.concatenate(p_rows[hh], axis=0).astype(BF16), vbs[hh], preferred_element_type=F32)
                    for hh in heads]

    s_olds = [s_scr[hh] for hh in heads]
    outs = [jnp.dot((qs[hh] * jnp.exp(bs[hh])).astype(BF16), s_olds[hh].astype(BF16), preferred_element_type=F32)
            + o_intras[hh] for hh in heads]
    for hh in heads:
        o = outs[hh]
        on = o * lax.rsqrt(jnp.mean(o * o, axis=-1, keepdims=True) + EPS) * gn_ref[...]
        og_ref[:, dv(hh)] = (on * _silu(gr_ref[:, dv(hh)])).astype(og_ref.dtype)

    b_ends = [b[c - 1:c] for b in bs]
    khats = [(ks[hh] * jnp.exp(b_ends[hh] - bs[hh])).astype(BF16) for hh in heads]
    d_ss = [lax.dot_general(khats[hh], vbs[hh], TN_DIMS, preferred_element_type=F32) for hh in heads]
    decays = []
    for hh in heads:
        pieces = jnp.concatenate(list(_split3(jnp.exp(b_ends[hh]))) + [jnp.zeros((5, GLA_DK), F32)], axis=0)
        decays.append(lax.dot_general(pieces.astype(BF16), ones8, TN_DIMS, preferred_element_type=F32))
    for hh in heads:
        s_scr[hh] = s_olds[hh] * decays[hh] + d_ss[hh]

    @pl.when(n == n_chunks - 1)
    def _():
        so_ref[...] = s_scr[...]


def _gla(layer, grp, proj, w2, ba, gn, s0, states, out_dtype):
    chunk, sub = (64, 8) if grp.dense else (grp.seq_rows, grp.seq_rows)
    nc = grp.seq_rows // chunk
    has_state, chained = s0 is not None, states is not None
    body = functools.partial(_gla_body, chunk=chunk, sub=sub, valid=min(chunk, grp.valid_rows), n_chunks=nc,
                             has_state=has_state, chained=chained)
    state_spec = _layer_spec(layer, (None, GLA_H, GLA_DK, GLA_DV), lambda b, n: (b, 0, 0, 0))
    in_specs = [
        pl.BlockSpec((chunk, 2048), lambda b, n: (b * nc + n, 0)),
        pl.BlockSpec((chunk, 2048), lambda b, n: (b * nc + n, COL_GV // 2048)),
        pl.BlockSpec((chunk, 2048), lambda b, n: (b * nc + n, COL_GR // 2048)),
        pl.BlockSpec((chunk, GLR_BLOCK), lambda b, n: (b * nc + n, COL_GLR // GLR_BLOCK)),
        _layer_spec(layer, (GATE_RANK, GLA_H * GLA_DK), lambda b, n: (0, 0)),
        _layer_spec(layer, (1, GLA_H * GLA_DK), lambda b, n: (0, 0)),
        _layer_spec(layer, (1, GLA_DV), lambda b, n: (0, 0)),
    ]
    args = [proj, proj, proj, proj, w2, ba, gn]
    if has_state:
        in_specs.append(state_spec)
        args.append(s0)
    if chained:
        in_specs.append(pl.BlockSpec(memory_space=pl.ANY))
        args.append(states)
    return pl.pallas_call(
        body,
        grid=(grp.n_seq, nc),
        in_specs=in_specs,
        out_specs=[pl.BlockSpec((chunk, 2048), lambda b, n: (b * nc + n, 0)), state_spec],
        out_shape=[
            jax.ShapeDtypeStruct((grp.rows, 2048), out_dtype),
            jax.ShapeDtypeStruct((DEPTH, grp.n_seq, GLA_H, GLA_DK, GLA_DV), F32),
        ],
        input_output_aliases={len(args) - 1: 1} if chained else {},
        scratch_shapes=[pltpu.VMEM((GLA_H, GLA_DK, GLA_DV), F32)],
        compiler_params=_params(2, 40),
        name="gla",
    )(*args)


def _sink_attention(problems, bias):
    sts = [lax.dot_general(k, q, NT_DIMS, preferred_element_type=F32) + bias for q, k, _, _ in problems]
    ms = [jnp.maximum(jnp.max(st, axis=0, keepdims=True), sink) for st, (_, _, _, sink) in zip(sts, problems)]
    ps = [jnp.exp(st - m) for st, m in zip(sts, ms)]
    dens = [jnp.sum(p, axis=0, keepdims=True) + jnp.exp(sink - m) for p, m, (_, _, _, sink) in zip(ps, ms, problems)]
    ots = [lax.dot_general(v, p.astype(BF16), TN_DIMS, preferred_element_type=F32)
           for p, (_, _, v, _) in zip(ps, problems)]
    return [(ot / den).T for ot, den in zip(ots, dens)]


def _kv_head(parts, kv, scale=None):
    cols = slice(kv * SWA_HD, (kv + 1) * SWA_HD)
    x = jnp.concatenate([r if r.shape[1] == SWA_HD else r[:, cols] for r in parts], axis=0)
    return (x if scale is None else x * scale).astype(BF16)


def _swa_prompt_body(sinks_ref, q_ref, kp_ref, kc_ref, vp_ref, vc_ref, o_ref, *, layer):
    i = pl.program_id(1)
    w = WINDOW
    kpos = lax.broadcasted_iota(jnp.int32, (2 * w, w), 0)
    dist = lax.broadcasted_iota(jnp.int32, (2 * w, w), 1) + w - kpos
    bias = jnp.where((dist >= 0) & (dist < w) & ((kpos >= w) | (i > 0)), 0.0, -jnp.inf)
    for kv in range(SWA_KV):
        k2 = _kv_head([kp_ref, kc_ref], kv, SWA_HD ** -0.5)
        v2 = _kv_head([vp_ref, vc_ref], kv)
        heads = range(kv * SWA_G, (kv + 1) * SWA_G)
        problems = [(q_ref[:, h * SWA_HD:(h + 1) * SWA_HD].astype(BF16), k2, v2, sinks_ref[layer, h]) for h in heads]
        for h, o in zip(heads, _sink_attention(problems, bias)):
            o_ref[:, h * SWA_HD:(h + 1) * SWA_HD] = o.astype(o_ref.dtype)


def _swa_prompt(layer, grp, proj, sinks):
    w = WINDOW
    nb = grp.seq_rows // w
    kvw = SWA_KV * SWA_HD
    cur = lambda col: (lambda b, i: (b * nb + i, col))
    prev = lambda col: (lambda b, i: (b * nb + jnp.maximum(i - 1, 0), col))
    return pl.pallas_call(
        functools.partial(_swa_prompt_body, layer=layer),
        grid=(grp.n_seq, nb),
        in_specs=[
            pl.BlockSpec(memory_space=pltpu.SMEM),
            pl.BlockSpec((w, 2048), cur(COL_SQ // 2048)),
            pl.BlockSpec((w, kvw), prev(COL_SK // kvw)),
            pl.BlockSpec((w, kvw), cur(COL_SK // kvw)),
            pl.BlockSpec((w, kvw), prev(COL_SV // kvw)),
            pl.BlockSpec((w, kvw), cur(COL_SV // kvw)),
        ],
        out_specs=pl.BlockSpec((w, 2048), lambda b, i: (b * nb + i, 0)),
        out_shape=jax.ShapeDtypeStruct((grp.rows, 2048), BF16),
        compiler_params=_params(2, 40),
        name="swa_prompt",
    )(sinks, proj, proj, proj, proj, proj)


def _swa_sample_body(sinks_ref, q_ref, kn_ref, vn_ref, ck_ref, cv_ref, o_ref, *, layer, rows, valid):
    w = WINDOW
    shape = (2 * w, SWA_G * rows)
    qi = lax.broadcasted_iota(jnp.int32, shape, 1) & (rows - 1)
    key = lax.broadcasted_iota(jnp.int32, shape, 0)
    visible = ((key < w) & (key > qi)) | ((key >= w) & (key - w <= qi) & (key < w + valid))
    bias = jnp.where(visible, 0.0, -jnp.inf)
    pad = jnp.zeros((w - rows, SWA_HD), F32)
    problems = []
    for kv in range(SWA_KV):
        heads = range(kv * SWA_G, (kv + 1) * SWA_G)
        k2 = _kv_head([ck_ref, kn_ref, pad], kv, SWA_HD ** -0.5)
        v2 = _kv_head([cv_ref, vn_ref, pad], kv)
        q = jnp.concatenate([q_ref[:, h * SWA_HD:(h + 1) * SWA_HD] for h in heads], axis=0).astype(BF16)
        sink = jnp.concatenate([jnp.full((1, rows), sinks_ref[layer, h], F32) for h in heads], axis=1)
        problems.append((q, k2, v2, sink))
    for kv, o in enumerate(_sink_attention(problems, bias)):
        for g in range(SWA_G):
            h = kv * SWA_G + g
            o_ref[:, h * SWA_HD:(h + 1) * SWA_HD] = o[g * rows:(g + 1) * rows].astype(o_ref.dtype)


def _swa_sample(layer, grp, proj, cache_k, cache_v, sinks):
    w = WINDOW
    rows = grp.seq_rows
    kvw = SWA_KV * SWA_HD
    body = functools.partial(_swa_sample_body, layer=layer, rows=rows, valid=grp.valid_rows)
    cache_spec = _layer_spec(layer, (None, w, kvw), lambda b: (b, 0, 0))
    return pl.pallas_call(
        body,
        grid=(grp.n_seq,),
        in_specs=[
            pl.BlockSpec(memory_space=pltpu.SMEM),
            pl.BlockSpec((rows, 2048), lambda b: (b, COL_SQ // 2048)),
            pl.BlockSpec((rows, kvw), lambda b: (b, COL_SK // kvw)),
            pl.BlockSpec((rows, kvw), lambda b: (b, COL_SV // kvw)),
            cache_spec,
            cache_spec,
        ],
        out_specs=pl.BlockSpec((rows, 2048), lambda b: (b, 0)),
        out_shape=jax.ShapeDtypeStruct((grp.rows, 2048), F32),
        compiler_params=_params(1, 32),
        name="swa_sample",
    )(sinks, proj, proj, proj, cache_k, cache_v)


def _merge_body(og_ref, os_ref, ga_ref, gb_ref, x_ref, gt_ref, w_ref, o_ref, m_scr):
    @pl.when(pl.program_id(1) == 0)
    def _():
        merged = (jax.nn.sigmoid(ga_ref[...]) * og_ref[...].astype(F32)
                  + jax.nn.sigmoid(gb_ref[...]) * os_ref[...].astype(F32))
        m_scr[...] = merged.astype(BF16)

    y = jnp.dot(m_scr[...], w_ref[...], preferred_element_type=F32)
    o_ref[...] = x_ref[...] + gt_ref[...] * y


def _merge_out(layer, grp, og, os_, proj, x, mod, w_out):
    tm = 512 if grp.dense else 256
    tn = 1024
    return pl.pallas_call(
        _merge_body,
        grid=(grp.rows // tm, D // tn),
        in_specs=[
            pl.BlockSpec((tm, D), lambda i, j: (i, 0)),
            pl.BlockSpec((tm, D), lambda i, j: (i, 0)),
            pl.BlockSpec((tm, D), lambda i, j: (i, COL_GA // D)),
            pl.BlockSpec((tm, D), lambda i, j: (i, COL_GB // D)),
            pl.BlockSpec((tm, tn), lambda i, j: (i, j)),
            _mod_spec(grp, tm, 2, tn=tn, with_j=True),
            _layer_spec(layer, (D, tn), lambda i, j: (0, j)),
        ],
        out_specs=pl.BlockSpec((tm, tn), lambda i, j: (i, j)),
        out_shape=jax.ShapeDtypeStruct((grp.rows, D), F32),
        scratch_shapes=[pltpu.VMEM((tm, D), BF16)],
        compiler_params=_params(2, 52),
        name="merge_out",
    )(og, os_, proj, proj, x, mod, w_out)


ROUTE_TM = 256
ROUTE_LANES = 128
SIDES_PER_PASS = 2
HEADS_PER_PASS = 2

_SLABS = (("a", 0, 16), ("a", 1, 8), ("b", 0, 16), ("b", 1, 8), ("a", 2, 8), ("a", 3, 8), ("a", 4, 8))


def _candidate_tables():
    flat, ok, seen = [], [], set()
    for kind, fixed, n in _SLABS:
        for r in range(n):
            a, b = (fixed, r) if kind == "a" else (r, fixed)
            good = (a + 1) * (b + 1) <= TOPK and (a, b) not in seen
            if good:
                seen.add((a, b))
            flat.append(a * TOPK + b if good else 1e9)
            ok.append(1.0 if good else 0.0)
    assert len(seen) == 50
    tile = lambda v: np.tile(np.asarray(v, np.float32)[:, None], (1, ROUTE_LANES))
    return tile(flat), tile(ok)


N_CAND = sum(n for _, _, n in _SLABS)


def _take_top(chains, tie_key, n_take, exact):
    vals = [v for v, _ in chains]
    for t in range(n_take):
        for i, (_, on_take) in enumerate(chains):
            m = jnp.max(vals[i], axis=0, keepdims=True)
            taken = vals[i] == m
            if exact:
                first = jnp.min(jnp.where(taken, tie_key, 2e9), axis=0, keepdims=True)
                taken = tie_key == first
            on_take(t, m, taken)
            vals[i] = jnp.where(taken, -jnp.inf, vals[i])


def _tied(taken_count):
    return jnp.max(taken_count) > float(TOPK)


def _route_body(x_ref, g_ref, sc_ref, sh_ref, wq_ref, keys_ref, flat_ref, ok_ref,
                h_ref, r2_ref, e2_ref, n1_ref, e1_ref, st_scr, rk_scr, tv_scr, sel_scr):
    h = _rms_modulate(x_ref[...], g_ref[...], sc_ref[...], sh_ref[...])
    h_ref[...] = h.T.astype(BF16)
    qp = jnp.dot(h.astype(BF16), wq_ref[...], preferred_element_type=F32)
    n_sub = ROUTE_TM // ROUTE_LANES
    for hc in range(2 * PEER_H):
        st = lax.dot_general(keys_ref[hc], qp[:, hc * N_KEYS:(hc + 1) * N_KEYS].astype(BF16), NT_DIMS,
                             preferred_element_type=F32)
        for u in range(n_sub):
            st_scr[hc, u] = st[:, u * ROUTE_LANES:(u + 1) * ROUTE_LANES]

    key_id = lax.broadcasted_iota(jnp.int32, (N_KEYS, ROUTE_LANES), 0).astype(F32)

    def side_ranks(blocks, exact):
        states = [{"rank": jnp.full((N_KEYS, ROUTE_LANES), 99.0, F32), "tops": []} for _ in blocks]

        def recorder(state):
            def on_take(t, m, taken):
                state["rank"] = jnp.where(taken, float(t), state["rank"])
                state["tops"].append(m)
            return on_take

        _take_top([(st_scr[hc, u], recorder(st)) for (hc, u), st in zip(blocks, states)], key_id, TOPK, exact)
        for (hc, u), st in zip(blocks, states):
            rk_scr[hc, u] = st["rank"]
            tv_scr[hc, u] = jnp.concatenate(st["tops"], axis=0)
        return [jnp.sum(jnp.where(st["rank"] < 99.0, 1.0, 0.0), axis=0, keepdims=True) for st in states]

    def side_topk(i, carry):
        blocks = [(i * SIDES_PER_PASS + d, u) for d in range(SIDES_PER_PASS) for u in range(n_sub)]
        counts = side_ranks(blocks, exact=False)

        @pl.when(_tied(jnp.concatenate(counts, axis=0)))
        def _():
            side_ranks(blocks, exact=True)
        return carry

    lax.fori_loop(0, 2 * PEER_H // SIDES_PER_PASS, side_topk, 0)

    flat, ok = flat_ref[...], ok_ref[...] > 0.0
    row16 = lax.broadcasted_iota(jnp.int32, (TOPK, ROUTE_LANES), 0)

    def joint_topk(i, carry):
        blocks = [(i * HEADS_PER_PASS + d, u) for d in range(HEADS_PER_PASS) for u in range(n_sub)]

        def candidates(h, u):
            t1, t2 = tv_scr[2 * h, u], tv_scr[2 * h + 1, u]
            slabs = []
            for kind, fixed, n in _SLABS:
                slabs.append(t1[fixed:fixed + 1] + t2[0:n] if kind == "a" else t1[0:n] + t2[fixed:fixed + 1])
            return jnp.where(ok, jnp.concatenate(slabs, axis=0), -jnp.inf)

        def select(exact):
            states = [{"sel": jnp.zeros((N_CAND, ROUTE_LANES), F32)} for _ in blocks]

            def recorder(state):
                def on_take(t, m, taken):
                    state["sel"] = jnp.where(taken, 1.0, state["sel"])
                return on_take

            _take_top([(candidates(h, u), recorder(st)) for (h, u), st in zip(blocks, states)], flat, TOPK, exact)
            for k, st in enumerate(states):
                sel_scr[k] = st["sel"]
            return [jnp.sum(st["sel"], axis=0, keepdims=True) for st in states]

        counts = select(exact=False)

        @pl.when(_tied(jnp.concatenate(counts, axis=0)))
        def _():
            select(exact=True)

        for k, (h, u) in enumerate(blocks):
            t1, t2 = tv_scr[2 * h, u], tv_scr[2 * h + 1, u]
            cand, sel = candidates(h, u), sel_scr[k]
            z = jnp.sum(jnp.where(sel > 0.0, jnp.exp(cand - cand[0:1]), 0.0), axis=0, keepdims=True)
            rank1, rank2 = rk_scr[2 * h, u], rk_scr[2 * h + 1, u]
            slab, start = {}, 0
            for kind, fixed, n in _SLABS:
                slab[kind, fixed] = sel[start:start + n]
                start += n
            n_all = slab["b", 0] + jnp.concatenate([slab["b", 1], jnp.zeros((TOPK - 8, ROUTE_LANES), F32)], axis=0)
            for kind, fixed, _ in _SLABS:
                if kind == "a":
                    n_all = n_all + jnp.where(row16 == fixed, jnp.sum(slab[kind, fixed], axis=0, keepdims=True), 0.0)
            n1 = jnp.zeros((N_KEYS, ROUTE_LANES), F32)
            for a in range(TOPK):
                n1 = jnp.where(rank1 == float(a), n_all[a:a + 1], n1)
            lanes = slice(u * ROUTE_LANES, (u + 1) * ROUTE_LANES)
            r2_ref[h, 0, :, lanes] = rank2.astype(r2_ref.dtype)
            n1_ref[h, 0, :, lanes] = n1
            e1_ref[h, 0, :, lanes] = jnp.exp(st_scr[2 * h, u] - t1[0:1]) / z
            e2_ref[h, 0, :, lanes] = jnp.exp(st_scr[2 * h + 1, u] - t2[0:1]).astype(e2_ref.dtype)
        return carry

    lax.fori_loop(0, PEER_H // HEADS_PER_PASS, joint_topk, 0)


def _route(layer, grp, x, g, mod, wq, keys, tables):
    tm = ROUTE_TM
    n_sub = tm // ROUTE_LANES
    n_tiles = grp.rows // tm
    table_spec = pl.BlockSpec((N_CAND, ROUTE_LANES), lambda i: (0, 0))
    route_spec = pl.BlockSpec((PEER_H, 1, N_KEYS, tm), lambda i: (0, i, 0, 0))
    tile_shape = jax.ShapeDtypeStruct((PEER_H, n_tiles, N_KEYS, tm), BF16)
    row_shape = jax.ShapeDtypeStruct((PEER_H, n_tiles, N_KEYS, tm), F32)
    return pl.pallas_call(
        _route_body,
        grid=(n_tiles,),
        in_specs=[
            pl.BlockSpec((tm, D), lambda i: (i, 0)),
            _layer_spec(layer, (1, D), lambda i: (0, 0)),
            _mod_spec(grp, tm, 4),
            _mod_spec(grp, tm, 3),
            _layer_spec(layer, (D, D), lambda i: (0, 0)),
            _layer_spec(layer, (2 * PEER_H, N_KEYS, N_KEYS), lambda i: (0, 0, 0)),
            table_spec, table_spec,
        ],
        out_specs=[pl.BlockSpec((D, tm), lambda i: (0, i)), route_spec, route_spec, route_spec, route_spec],
        out_shape=[jax.ShapeDtypeStruct((D, grp.rows), BF16), tile_shape, tile_shape, row_shape, row_shape],
        scratch_shapes=[
            pltpu.VMEM((2 * PEER_H, n_sub, N_KEYS, ROUTE_LANES), F32),
            pltpu.VMEM((2 * PEER_H, n_sub, N_KEYS, ROUTE_LANES), F32),
            pltpu.VMEM((2 * PEER_H, n_sub, TOPK, ROUTE_LANES), F32),
            pltpu.VMEM((HEADS_PER_PASS * n_sub, N_CAND, ROUTE_LANES), F32),
        ],
        compiler_params=_params(1, 48),
        name="peer_route",
    )(x, g, mod, mod, wq, keys, *tables)


PEER_TT = 1024
PEER_CH = ROUTE_TM
PEER_ET = 512


def _peer_gates(r2_ref, e2_ref, n1_ref, e1_ref, g_scr, slot, step):
    n_blk = PEER_ET // N_KEYS
    for ch in range(PEER_TT // PEER_CH):
        for ib in range(n_blk):
            i1 = step * n_blk + ib
            gate = jnp.zeros((N_KEYS, PEER_CH), BF16)
            for h in range(PEER_H):
                keep = r2_ref[h, ch] < n1_ref[h, ch, pl.ds(i1, 1), :].astype(BF16)
                weight = e2_ref[h, ch] * e1_ref[h, ch, pl.ds(i1, 1), :].astype(BF16)
                gate = gate + jnp.where(keep, weight, jnp.zeros_like(weight))
            g_scr[slot, ch, ib * N_KEYS:(ib + 1) * N_KEYS, :] = gate


def _peer_body(h_ref, u_ref, v_ref, r2_ref, e2_ref, n1_ref, e1_ref, o_ref, g_scr):
    j = pl.program_id(1)
    last = pl.num_programs(1) - 1
    slot = j % 2
    tables = (r2_ref, e2_ref, n1_ref, e1_ref)

    @pl.when(j == 0)
    def _():
        o_ref[...] = jnp.zeros_like(o_ref)
        _peer_gates(*tables, g_scr, 0, 0)

    _peer_gates(*tables, g_scr, 1 - slot, jnp.minimum(j + 1, last))

    u, v = u_ref[...], v_ref[...]
    n_ch = PEER_TT // PEER_CH
    quarter_e, half_d = PEER_ET // 4, D // 2
    chunk = lambda ch: slice(ch * PEER_CH, (ch + 1) * PEER_CH)
    acts = [[jnp.dot(u[s * quarter_e:(s + 1) * quarter_e], h_ref[:, chunk(ch)], preferred_element_type=F32)
             for s in range(4)] for ch in range(n_ch)]
    ps = []
    for ch in range(n_ch):
        parts = []
        for s in range(4):
            act = acts[ch][s]
            gelu = 0.5 * act * (1.0 + lax.erf(act * (2.0 ** -0.5)))
            parts.append(g_scr[slot, ch, s * quarter_e:(s + 1) * quarter_e, :] * gelu.astype(BF16))
        ps.append(jnp.concatenate(parts, axis=0))
    for ch in range(n_ch):
        for s in range(2):
            rows = slice(s * half_d, (s + 1) * half_d)
            o_ref[rows, chunk(ch)] += lax.dot_general(v[:, rows], ps[ch], TN_DIMS, preferred_element_type=F32)


def _peer(layer, grp, h2t, routing, u, v):
    tt, et = PEER_TT, PEER_ET
    n_ch = tt // PEER_CH
    once = pl.Buffered(1)
    route_spec = pl.BlockSpec((PEER_H, n_ch, N_KEYS, PEER_CH), lambda i, j: (0, i, 0, 0), pipeline_mode=once)
    return pl.pallas_call(
        _peer_body,
        grid=(grp.rows // tt, N_EXPERTS // et),
        in_specs=[
            pl.BlockSpec((D, tt), lambda i, j: (0, i), pipeline_mode=once),
            _layer_spec(layer, (et, D), lambda i, j: (j, 0)),
            _layer_spec(layer, (et, D), lambda i, j: (j, 0)),
            route_spec, route_spec, route_spec, route_spec,
        ],
        out_specs=pl.BlockSpec((D, tt), lambda i, j: (0, i)),
        out_shape=jax.ShapeDtypeStruct((D, grp.rows), F32),
        scratch_shapes=[pltpu.VMEM((2, n_ch, et, PEER_CH), BF16)],
        compiler_params=_params(2, 56),
        name="peer_dense",
    )(h2t, u, v, *routing)


def _peer_residual_body(f_ref, x_ref, gt_ref, fg_ref, o_ref, *, final_norm):
    x2 = x_ref[...] + gt_ref[...] * f_ref[...].T
    if final_norm:
        x2 = x2 * lax.rsqrt(jnp.mean(x2 * x2, axis=-1, keepdims=True) + EPS) * fg_ref[...]
    o_ref[...] = x2


def _peer_residual(grp, ffn_t, x, mod, final_g, final_norm):
    tm = 256
    return pl.pallas_call(
        functools.partial(_peer_residual_body, final_norm=final_norm),
        grid=(grp.rows // tm,),
        in_specs=[
            pl.BlockSpec((D, tm), lambda i: (0, i)),
            pl.BlockSpec((tm, D), lambda i: (i, 0)),
            _mod_spec(grp, tm, 5),
            pl.BlockSpec((1, D), lambda i: (0, 0)),
        ],
        out_specs=pl.BlockSpec((tm, D), lambda i: (i, 0)),
        out_shape=jax.ShapeDtypeStruct((grp.rows, D), F32),
        compiler_params=_params(1, 32),
        name="peer_residual",
    )(ffn_t, x, mod, final_g)


def _proj_weights(w_in):
    gla_end, glr_end = 6144, 6160
    sq, sk, sv, ga, gb = (w_in[..., a:b] for a, b in ((6160, 8208), (8208, 8464), (8464, 8720), (8720, 10768),
                                                       (10768, 12816)))
    pad = jnp.zeros(w_in.shape[:-1] + (N_PROJ - COL_GLR - GATE_RANK,), w_in.dtype)
    return jnp.concatenate([w_in[..., :gla_end], sq, ga, gb, sk, sv, w_in[..., gla_end:glr_end], pad],
                           axis=-1).astype(BF16)


def kernel(x_prompt, x_sample, state_gla, cache_swa_k, cache_swa_v, c_prompt, c_sample, w_ada, b_ada, norm1_g,
           norm2_g, w_in, w_alpha2, b_alpha, gla_norm_g, swa_sinks, w_out, peer_wq, peer_keys, peer_u, peer_v,
           final_g):
    n_p, l_p = x_prompt.shape[0], x_prompt.shape[1]
    n_s, l_s = x_sample.shape[0], x_sample.shape[1]
    prompt = _Group(n_p, l_p, l_p)
    sample = _Group(n_s, SAMPLE_SEQ_ROWS, l_s)
    kvw = SWA_KV * SWA_HD
    xs = {
        prompt: x_prompt.reshape(prompt.rows, D),
        sample: jnp.pad(x_sample, ((0, 0), (0, sample.seq_rows - l_s), (0, 0))).reshape(sample.rows, D),
    }
    c_all = jnp.concatenate([c_prompt, c_sample], axis=0)
    tables = tuple(jnp.asarray(t) for t in _candidate_tables())
    fg = final_g.reshape(1, D)

    row = lambda a: a.reshape(DEPTH, 1, -1)
    b_ada_r, g1, g2, ba, gn = row(b_ada), row(norm1_g), row(norm2_g), row(b_alpha), row(gla_norm_g)
    w_cat = _proj_weights(w_in)
    w2, w_o, wq = w_alpha2.astype(BF16), w_out.astype(BF16), peer_wq.astype(BF16)
    keys = peer_keys.reshape(DEPTH, 2 * PEER_H, N_KEYS, N_KEYS).astype(BF16)
    u, v = peer_u.astype(BF16), peer_v.astype(BF16)
    cache_k = cache_swa_k.reshape(DEPTH, n_s, WINDOW, kvw)
    cache_v = cache_swa_v.reshape(DEPTH, n_s, WINDOW, kvw)

    states = {prompt: None, sample: None}
    k_bufs, v_bufs = {prompt: [], sample: []}, {prompt: [], sample: []}
    for l in range(DEPTH):
        mod = _ada_mod(l, c_all, w_ada, b_ada_r)
        mods = {
            prompt: mod[:n_p].reshape(n_p, 1, 6 * D),
            sample: jnp.repeat(mod[n_p:], sample.seq_rows, axis=0).reshape(1, sample.rows, 6 * D),
        }
        for grp in (prompt, sample):
            x, m = xs[grp], mods[grp]
            proj = _norm_matmul(l, grp, x, g1, m, w_cat)
            heads = lambda a: a.reshape(grp.n_seq, grp.seq_rows, SWA_KV, SWA_HD)
            k_new, v_new = heads(proj[:, COL_SK:COL_SK + kvw]), heads(proj[:, COL_SV:COL_SV + kvw])
            if grp.dense:
                og, states[grp] = _gla(l, grp, proj, w2, ba, gn, None, states[grp], BF16)
                os_ = _swa_prompt(l, grp, proj, swa_sinks)
                k_buf, v_buf = k_new[:, -WINDOW:], v_new[:, -WINDOW:]
            else:
                og, states[grp] = _gla(l, grp, proj, w2, ba, gn, state_gla, states[grp], F32)
                os_ = _swa_sample(l, grp, proj, cache_k, cache_v, swa_sinks)
                k_buf = jnp.concatenate([cache_swa_k[l][:, l_s:], k_new[:, :l_s]], axis=1)
                v_buf = jnp.concatenate([cache_swa_v[l][:, l_s:], v_new[:, :l_s]], axis=1)
            x1 = _merge_out(l, grp, og, os_, proj, x, m, w_o)
            h2t, *routing = _route(l, grp, x1, g2, m, wq, keys, tables)
            ffn_t = _peer(l, grp, h2t, routing, u, v)
            xs[grp] = _peer_residual(grp, ffn_t, x1, m, fg, final_norm=(l == DEPTH - 1))
            k_bufs[grp].append(k_buf)
            v_bufs[grp].append(v_buf)

    y_prompt = xs[prompt].reshape(x_prompt.shape)
    y_sample = xs[sample].reshape(n_s, sample.seq_rows, D)[:, :l_s]
    caches = lambda grp: (states[grp], jnp.stack(k_bufs[grp]), jnp.stack(v_bufs[grp]))
    return (y_prompt, y_sample) + caches(prompt) + caches(sample)
```

```python
import functools
from typing import NamedTuple

import numpy as np
import jax
import jax.numpy as jnp
from jax import lax
from jax.experimental import pallas as pl
from jax.experimental.pallas import tpu as pltpu

F32 = jnp.float32
BF16 = jnp.bfloat16

D = 2048
DEPTH = 2
GLA_H, GLA_DK, GLA_DV, GATE_RANK, GATE_TEMP = 4, 256, 512, 16, 16.0
SWA_HD, SWA_KV, SWA_G, WINDOW = 64, 4, 8, 128
PEER_H, N_KEYS, TOPK = 8, 128, 16
N_EXPERTS = N_KEYS * N_KEYS
EPS = 1e-6

COL_GQ, COL_GK, COL_GV, COL_GR = 0, 1024, 2048, 4096
COL_SQ, COL_GA, COL_GB = 6144, 8192, 10240
COL_SK, COL_SV, COL_GLR = 12288, 12544, 12800
N_PROJ = 13312
GLR_BLOCK = 128

MIB = 1024 * 1024
NT_DIMS = (((1,), (1,)), ((), ()))
TN_DIMS = (((0,), (0,)), ((), ()))


class _Group(NamedTuple):
    n_seq: int
    seq_rows: int
    valid_rows: int

    @property
    def rows(self):
        return self.n_seq * self.seq_rows

    @property
    def dense(self):
        return self.valid_rows == self.seq_rows


SAMPLE_SEQ_ROWS = 8


def _params(n_grid, vmem_mib):
    return pltpu.CompilerParams(dimension_semantics=("arbitrary",) * n_grid, vmem_limit_bytes=vmem_mib * MIB)


def _mod_spec(grp, tm, comp, tn=D, with_j=False):
    per_comp = D // tn
    if grp.seq_rows >= tm:
        per_seq = grp.seq_rows // tm
        if with_j:
            return pl.BlockSpec((None, 1, tn), lambda i, j: (i // per_seq, 0, comp * per_comp + j))
        return pl.BlockSpec((None, 1, tn), lambda i, *_: (i // per_seq, 0, comp * per_comp))
    if with_j:
        return pl.BlockSpec((None, tm, tn), lambda i, j: (0, i, comp * per_comp + j))
    return pl.BlockSpec((None, tm, tn), lambda i, *_: (0, i, comp * per_comp))


def _layer_spec(layer, block, index_map):
    return pl.BlockSpec((None,) + tuple(block), lambda *grid: (layer,) + tuple(index_map(*grid)))


def _rms_modulate(x, g, sc, sh):
    xn = x * lax.rsqrt(jnp.mean(x * x, axis=-1, keepdims=True) + EPS)
    return xn * g * (1.0 + sc) + sh


def _silu(x):
    return x * jax.nn.sigmoid(x)


def _ada_body(c_ref, w_ref, b_ref, o_ref):
    a = _silu(c_ref[...]).astype(BF16)
    o_ref[...] = jnp.dot(a, w_ref[...].astype(BF16), preferred_element_type=F32) + b_ref[...]


def _ada_mod(layer, c_all, w_ada, b_ada):
    rows, tn = c_all.shape[0], 1024
    return pl.pallas_call(
        _ada_body,
        grid=(6 * D // tn,),
        in_specs=[
            pl.BlockSpec((rows, D), lambda j: (0, 0)),
            _layer_spec(layer, (D, tn), lambda j: (0, j)),
            _layer_spec(layer, (1, tn), lambda j: (0, j)),
        ],
        out_specs=pl.BlockSpec((rows, tn), lambda j: (0, j)),
        out_shape=jax.ShapeDtypeStruct((rows, 6 * D), F32),
        compiler_params=_params(1, 40),
        name="ada_mod",
    )(c_all, w_ada, b_ada)


def _norm_mm_body(x_ref, g_ref, sc_ref, sh_ref, w_ref, o_ref, h_scr):
    @pl.when(pl.program_id(1) == 0)
    def _():
        h_scr[...] = _rms_modulate(x_ref[...], g_ref[...], sc_ref[...], sh_ref[...]).astype(BF16)

    o_ref[...] = jnp.dot(h_scr[...], w_ref[...], preferred_element_type=F32)


def _norm_matmul(layer, grp, x, g, mod, w_cat):
    tm = 1024 if grp.dense else 512
    tn = 1024
    n = w_cat.shape[-1]
    return pl.pallas_call(
        _norm_mm_body,
        grid=(grp.rows // tm, n // tn),
        in_specs=[
            pl.BlockSpec((tm, D), lambda i, j: (i, 0)),
            _layer_spec(layer, (1, D), lambda i, j: (0, 0)),
            _mod_spec(grp, tm, 1),
            _mod_spec(grp, tm, 0),
            _layer_spec(layer, (D, tn), lambda i, j: (0, j)),
        ],
        out_specs=pl.BlockSpec((tm, tn), lambda i, j: (i, j)),
        out_shape=jax.ShapeDtypeStruct((grp.rows, n), F32),
        scratch_shapes=[pltpu.VMEM((tm, D), BF16)],
        compiler_params=_params(2, 48),
        name="norm_proj",
    )(x, g, mod, mod, w_cat)


def _split3(x):
    hi = x.astype(BF16).astype(F32)
    r1 = x - hi
    mid = r1.astype(BF16).astype(F32)
    lo = (r1 - mid).astype(BF16).astype(F32)
    return hi, mid, lo


def _gla_body(qk_ref, v_ref, gr_ref, glr_ref, w2_ref, ba_ref, gn_ref, *rest, layer, chunk, sub, valid, n_chunks,
              has_state, chained):
    s0_ref = rest[0] if has_state else None
    og_ref, so_ref, s_scr = rest[int(has_state) + int(chained):]
    c = chunk
    n = pl.program_id(1)

    @pl.when(n == 0)
    def _():
        s_scr[...] = s0_ref[...] if has_state else jnp.zeros_like(s_scr)

    row = lax.broadcasted_iota(jnp.int32, (c, 1), 0)
    live = row < valid
    tri = (lax.broadcasted_iota(jnp.int32, (c, c), 1) <= lax.broadcasted_iota(jnp.int32, (c, c), 0)).astype(BF16)
    lane = lax.broadcasted_iota(jnp.int32, (sub, c), 1)
    rowi = lax.broadcasted_iota(jnp.int32, (sub, c), 0)
    glr = glr_ref[:, 0:GATE_RANK].astype(BF16)
    ones8 = jnp.ones((8, GLA_DV), BF16)

    heads = range(GLA_H)
    dk = lambda hh: slice(hh * GLA_DK, (hh + 1) * GLA_DK)
    dv = lambda hh: slice(hh * GLA_DV, (hh + 1) * GLA_DV)
    qs = [qk_ref[:, COL_GQ + hh * GLA_DK:COL_GQ + (hh + 1) * GLA_DK] * (GLA_DK ** -0.5) for hh in heads]
    ks = [qk_ref[:, COL_GK + hh * GLA_DK:COL_GK + (hh + 1) * GLA_DK] for hh in heads]
    vs = [v_ref[:, dv(hh)] for hh in heads]
    zs = [jnp.dot(glr, w2_ref[:, dk(hh)], preferred_element_type=F32) + ba_ref[:, dk(hh)] for hh in heads]
    gs = [(jnp.minimum(z, 0.0) - jnp.log(1.0 + jnp.exp(-jnp.abs(z)))) * (1.0 / GATE_TEMP) for z in zs]
    if valid < c:
        gs = [jnp.where(live, g, 0.0) for g in gs]
        ks = [jnp.where(live, k, 0.0) for k in ks]
    bs = [sum(jnp.dot(tri, p.astype(BF16), preferred_element_type=F32) for p in _split3(g)) for g in gs]
    vbs = [v.astype(BF16) for v in vs]

    o_intras = [jnp.zeros((c, GLA_DV), F32) for _ in heads]
    p_rows = [[] for _ in heads]
    for i in range(c // sub):
        r0 = i * sub
        for hh in heads:
            b, q, k, v = bs[hh], qs[hh], ks[hh], vs[hh]
            b_i, q_i, k_i = b[r0:r0 + sub], q[r0:r0 + sub], k[r0:r0 + sub]
            att = jnp.zeros((sub, c), F32)
            for s in range(min(sub, valid - r0)):
                e = jnp.exp(jnp.minimum(b_i - b_i[s:s + 1], 0.0))
                col = jnp.sum(q_i * k_i[s:s + 1] * e, axis=-1, keepdims=True)
                if c == sub:
                    o_intras[hh] = o_intras[hh] + jnp.where(row >= s, col, 0.0) * v[s:s + 1]
                else:
                    att = jnp.where(lane == r0 + s, col, att)
            if c == sub:
                continue
            att = jnp.where(lane <= rowi + r0, att, 0.0)
            if i > 0:
                ref = b[r0 - 1:r0]
                qt = (q_i * jnp.exp(b_i - ref)).astype(BF16)
                kt = (k * jnp.exp(jnp.minimum(ref - b, 0.0))).astype(BF16)
                off = lax.dot_general(qt, kt, NT_DIMS, preferred_element_type=F32)
                att = jnp.where(lane < r0, off, att)
            p_rows[hh].append(att)
    if c != sub:
        o_intras = [jnp.dot(jnp.concatenate(p_rows[hh], axis=0).astype(BF16), vbs[hh], preferred_element_type=F32)
                    for hh in heads]

    s_olds = [s_scr[hh] for hh in heads]
    outs = [jnp.dot((qs[hh] * jnp.exp(bs[hh])).astype(BF16), s_olds[hh].astype(BF16), preferred_element_type=F32)
            + o_intras[hh] for hh in heads]
    for hh in heads:
        o = outs[hh]
        on = o * lax.rsqrt(jnp.mean(o * o, axis=-1, keepdims=True) + EPS) * gn_ref[...]
        og_ref[:, dv(hh)] = (on * _silu(gr_ref[:, dv(hh)])).astype(og_ref.dtype)

    b_ends = [b[c - 1:c] for b in bs]
    khats = [(ks[hh] * jnp.exp(b_ends[hh] - bs[hh])).astype(BF16) for hh in heads]
    d_ss = [lax.dot_general(khats[hh], vbs[hh], TN_DIMS, preferred_element_type=F32) for hh in heads]
    decays = []
    for hh in heads:
        pieces = jnp.concatenate(list(_split3(jnp.exp(b_ends[hh]))) + [jnp.zeros((5, GLA_DK), F32)], axis=0)
        decays.append(lax.dot_general(pieces.astype(BF16), ones8, TN_DIMS, preferred_element_type=F32))
    for hh in heads:
        s_scr[hh] = s_olds[hh] * decays[hh] + d_ss[hh]

    @pl.when(n == n_chunks - 1)
    def _():
        if chained:
            so_ref[...] = s_scr[...]
        else:
            for l in range(DEPTH):
                so_ref[l] = s_scr[...] if l == layer else jnp.zeros_like(s_scr)


def _gla(layer, grp, proj, w2, ba, gn, s0, states, out_dtype):
    chunk, sub = (64, 8) if grp.dense else (grp.seq_rows, grp.seq_rows)
    nc = grp.seq_rows // chunk
    has_state, chained = s0 is not None, states is not None
    body = functools.partial(_gla_body, layer=layer, chunk=chunk, sub=sub, valid=min(chunk, grp.valid_rows),
                             n_chunks=nc, has_state=has_state, chained=chained)
    state_spec = _layer_spec(layer, (None, GLA_H, GLA_DK, GLA_DV), lambda b, n: (b, 0, 0, 0))
    all_layers_spec = pl.BlockSpec((DEPTH, None, GLA_H, GLA_DK, GLA_DV), lambda b, n: (0, b, 0, 0, 0))
    in_specs = [
        pl.BlockSpec((chunk, 2048), lambda b, n: (b * nc + n, 0)),
        pl.BlockSpec((chunk, 2048), lambda b, n: (b * nc + n, COL_GV // 2048)),
        pl.BlockSpec((chunk, 2048), lambda b, n: (b * nc + n, COL_GR // 2048)),
        pl.BlockSpec((chunk, GLR_BLOCK), lambda b, n: (b * nc + n, COL_GLR // GLR_BLOCK)),
        _layer_spec(layer, (GATE_RANK, GLA_H * GLA_DK), lambda b, n: (0, 0)),
        _layer_spec(layer, (1, GLA_H * GLA_DK), lambda b, n: (0, 0)),
        _layer_spec(layer, (1, GLA_DV), lambda b, n: (0, 0)),
    ]
    args = [proj, proj, proj, proj, w2, ba, gn]
    if has_state:
        in_specs.append(state_spec)
        args.append(s0)
    if chained:
        in_specs.append(pl.BlockSpec(memory_space=pl.ANY))
        args.append(states)
    return pl.pallas_call(
        body,
        grid=(grp.n_seq, nc),
        in_specs=in_specs,
        out_specs=[pl.BlockSpec((chunk, 2048), lambda b, n: (b * nc + n, 0)),
                   state_spec if chained else all_layers_spec],
        out_shape=[
            jax.ShapeDtypeStruct((grp.rows, 2048), out_dtype),
            jax.ShapeDtypeStruct((DEPTH, grp.n_seq, GLA_H, GLA_DK, GLA_DV), F32),
        ],
        input_output_aliases={len(args) - 1: 1} if chained else {},
        scratch_shapes=[pltpu.VMEM((GLA_H, GLA_DK, GLA_DV), F32)],
        compiler_params=_params(2, 40),
        name="gla",
    )(*args)


def _sink_attention(problems, bias):
    sts = [lax.dot_general(k, q, NT_DIMS, preferred_element_type=F32) + bias for q, k, _, _ in problems]
    ms = [jnp.maximum(jnp.max(st, axis=0, keepdims=True), sink) for st, (_, _, _, sink) in zip(sts, problems)]
    ps = [jnp.exp(st - m) for st, m in zip(sts, ms)]
    dens = [jnp.sum(p, axis=0, keepdims=True) + jnp.exp(sink - m) for p, m, (_, _, _, sink) in zip(ps, ms, problems)]
    ots = [lax.dot_general(v, p.astype(BF16), TN_DIMS, preferred_element_type=F32)
           for p, (_, _, v, _) in zip(ps, problems)]
    return [(ot / den).T for ot, den in zip(ots, dens)]


def _kv_head(parts, kv, scale=None):
    cols = slice(kv * SWA_HD, (kv + 1) * SWA_HD)
    x = jnp.concatenate([r if r.shape[1] == SWA_HD else r[:, cols] for r in parts], axis=0)
    return (x if scale is None else x * scale).astype(BF16)


def _swa_prompt_body(sinks_ref, q_ref, kp_ref, kc_ref, vp_ref, vc_ref, o_ref, *, layer):
    i = pl.program_id(1)
    w = WINDOW
    kpos = lax.broadcasted_iota(jnp.int32, (2 * w, w), 0)
    dist = lax.broadcasted_iota(jnp.int32, (2 * w, w), 1) + w - kpos
    bias = jnp.where((dist >= 0) & (dist < w) & ((kpos >= w) | (i > 0)), 0.0, -jnp.inf)
    for kv in range(SWA_KV):
        k2 = _kv_head([kp_ref, kc_ref], kv, SWA_HD ** -0.5)
        v2 = _kv_head([vp_ref, vc_ref], kv)
        heads = range(kv * SWA_G, (kv + 1) * SWA_G)
        problems = [(q_ref[:, h * SWA_HD:(h + 1) * SWA_HD].astype(BF16), k2, v2, sinks_ref[layer, h]) for h in heads]
        for h, o in zip(heads, _sink_attention(problems, bias)):
            o_ref[:, h * SWA_HD:(h + 1) * SWA_HD] = o.astype(o_ref.dtype)


def _swa_prompt(layer, grp, proj, sinks):
    w = WINDOW
    nb = grp.seq_rows // w
    kvw = SWA_KV * SWA_HD
    cur = lambda col: (lambda b, i: (b * nb + i, col))
    prev = lambda col: (lambda b, i: (b * nb + jnp.maximum(i - 1, 0), col))
    return pl.pallas_call(
        functools.partial(_swa_prompt_body, layer=layer),
        grid=(grp.n_seq, nb),
        in_specs=[
            pl.BlockSpec(memory_space=pltpu.SMEM),
            pl.BlockSpec((w, 2048), cur(COL_SQ // 2048)),
            pl.BlockSpec((w, kvw), prev(COL_SK // kvw)),
            pl.BlockSpec((w, kvw), cur(COL_SK // kvw)),
            pl.BlockSpec((w, kvw), prev(COL_SV // kvw)),
            pl.BlockSpec((w, kvw), cur(COL_SV // kvw)),
        ],
        out_specs=pl.BlockSpec((w, 2048), lambda b, i: (b * nb + i, 0)),
        out_shape=jax.ShapeDtypeStruct((grp.rows, 2048), BF16),
        compiler_params=_params(2, 40),
        name="swa_prompt",
    )(sinks, proj, proj, proj, proj, proj)


SWA_SAMPLE_SEQS = 4


def _swa_sample_body(sinks_ref, q_ref, kn_ref, vn_ref, ck_ref, cv_ref, o_ref, *, layer, rows, valid):
    w = WINDOW
    shape = (2 * w, SWA_G * rows)
    qi = lax.broadcasted_iota(jnp.int32, shape, 1) & (rows - 1)
    key = lax.broadcasted_iota(jnp.int32, shape, 0)
    visible = ((key < w) & (key > qi)) | ((key >= w) & (key - w <= qi) & (key < w + valid))
    bias = jnp.where(visible, 0.0, -jnp.inf)
    pad = jnp.zeros((w - rows, SWA_HD), F32)
    sinks = [jnp.concatenate([jnp.full((1, rows), sinks_ref[layer, kv * SWA_G + g], F32) for g in range(SWA_G)],
                             axis=1) for kv in range(SWA_KV)]
    problems = []
    for s in range(SWA_SAMPLE_SEQS):
        seq = slice(s * rows, (s + 1) * rows)
        for kv in range(SWA_KV):
            cols = slice(kv * SWA_HD, (kv + 1) * SWA_HD)
            k2 = jnp.concatenate([ck_ref[s, :, cols], kn_ref[seq, cols], pad], axis=0) * (SWA_HD ** -0.5)
            v2 = jnp.concatenate([cv_ref[s, :, cols], vn_ref[seq, cols], pad], axis=0)
            heads = range(kv * SWA_G, (kv + 1) * SWA_G)
            q = jnp.concatenate([q_ref[seq, h * SWA_HD:(h + 1) * SWA_HD] for h in heads], axis=0)
            problems.append((q.astype(BF16), k2.astype(BF16), v2.astype(BF16), sinks[kv]))
    for idx, o in enumerate(_sink_attention(problems, bias)):
        s, kv = divmod(idx, SWA_KV)
        for g in range(SWA_G):
            h = kv * SWA_G + g
            o_ref[s * rows:(s + 1) * rows, h * SWA_HD:(h + 1) * SWA_HD] = o[g * rows:(g + 1) * rows].astype(o_ref.dtype)


def _swa_sample(layer, grp, proj, cache_k, cache_v, sinks):
    w = WINDOW
    n = SWA_SAMPLE_SEQS
    rows = grp.seq_rows
    kvw = SWA_KV * SWA_HD
    body = functools.partial(_swa_sample_body, layer=layer, rows=rows, valid=grp.valid_rows)
    cache_spec = _layer_spec(layer, (n, w, kvw), lambda b: (b, 0, 0))
    return pl.pallas_call(
        body,
        grid=(grp.n_seq // n,),
        in_specs=[
            pl.BlockSpec(memory_space=pltpu.SMEM),
            pl.BlockSpec((n * rows, 2048), lambda b: (b, COL_SQ // 2048)),
            pl.BlockSpec((n * rows, kvw), lambda b: (b, COL_SK // kvw)),
            pl.BlockSpec((n * rows, kvw), lambda b: (b, COL_SV // kvw)),
            cache_spec,
            cache_spec,
        ],
        out_specs=pl.BlockSpec((n * rows, 2048), lambda b: (b, 0)),
        out_shape=jax.ShapeDtypeStruct((grp.rows, 2048), F32),
        compiler_params=_params(1, 32),
        name="swa_sample",
    )(sinks, proj, proj, proj, cache_k, cache_v)


def _merge_body(og_ref, os_ref, ga_ref, gb_ref, x_ref, gt_ref, w_ref, o_ref):
    merged = (jax.nn.sigmoid(ga_ref[...]) * og_ref[...].astype(F32)
              + jax.nn.sigmoid(gb_ref[...]) * os_ref[...].astype(F32)).astype(BF16)
    half = D // 2
    ys = [jnp.dot(merged, w_ref[:, s * half:(s + 1) * half], preferred_element_type=F32) for s in range(2)]
    for s, y in enumerate(ys):
        cols = slice(s * half, (s + 1) * half)
        o_ref[:, cols] = x_ref[:, cols] + gt_ref[:, cols] * y


def _merge_out(layer, grp, og, os_, proj, x, mod, w_out):
    tm = 512 if grp.dense else 256
    rows = lambda i: (i, 0)
    return pl.pallas_call(
        _merge_body,
        grid=(grp.rows // tm,),
        in_specs=[
            pl.BlockSpec((tm, D), rows),
            pl.BlockSpec((tm, D), rows),
            pl.BlockSpec((tm, D), lambda i: (i, COL_GA // D)),
            pl.BlockSpec((tm, D), lambda i: (i, COL_GB // D)),
            pl.BlockSpec((tm, D), rows),
            _mod_spec(grp, tm, 2),
            pl.BlockSpec((None, D, D), lambda i: (layer, 0, 0), pipeline_mode=pl.Buffered(1)),
        ],
        out_specs=pl.BlockSpec((tm, D), rows),
        out_shape=jax.ShapeDtypeStruct((grp.rows, D), F32),
        compiler_params=_params(1, 54),
        name="merge_out",
    )(og, os_, proj, proj, x, mod, w_out)


ROUTE_TM = 256
ROUTE_LANES = 128
SIDES_PER_PASS = 2
HEADS_PER_PASS = 2

_SLABS = (("a", 0, 16), ("a", 1, 8), ("b", 0, 16), ("b", 1, 8), ("a", 2, 8), ("a", 3, 8), ("a", 4, 8))


def _candidate_tables():
    flat, ok, seen = [], [], set()
    for kind, fixed, n in _SLABS:
        for r in range(n):
            a, b = (fixed, r) if kind == "a" else (r, fixed)
            good = (a + 1) * (b + 1) <= TOPK and (a, b) not in seen
            if good:
                seen.add((a, b))
            flat.append(a * TOPK + b if good else 1e9)
            ok.append(1.0 if good else 0.0)
    assert len(seen) == 50
    tile = lambda v: np.tile(np.asarray(v, np.float32)[:, None], (1, ROUTE_LANES))
    return tile(flat), tile(ok)


N_CAND = sum(n for _, _, n in _SLABS)


def _take_top(chains, tie_key, n_take, exact):
    vals = [v for v, _ in chains]
    for t in range(n_take):
        for i, (_, on_take) in enumerate(chains):
            m = jnp.max(vals[i], axis=0, keepdims=True)
            taken = vals[i] == m
            if exact:
                first = jnp.min(jnp.where(taken, tie_key, 2e9), axis=0, keepdims=True)
                taken = tie_key == first
            on_take(t, m, taken)
            vals[i] = jnp.where(taken, -jnp.inf, vals[i])


def _tied(taken_count):
    return jnp.max(taken_count) > float(TOPK)


def _route_body(x_ref, g_ref, sc_ref, sh_ref, wq_ref, keys_ref, flat_ref, ok_ref,
                h_ref, r2_ref, e2_ref, n1_ref, e1_ref, st_scr, rk_scr, tv_scr, sel_scr):
    h = _rms_modulate(x_ref[...], g_ref[...], sc_ref[...], sh_ref[...])
    h_ref[...] = h.T.astype(BF16)
    qp = jnp.dot(h.astype(BF16), wq_ref[...], preferred_element_type=F32)
    n_sub = ROUTE_TM // ROUTE_LANES
    for hc in range(2 * PEER_H):
        st = lax.dot_general(keys_ref[hc], qp[:, hc * N_KEYS:(hc + 1) * N_KEYS].astype(BF16), NT_DIMS,
                             preferred_element_type=F32)
        for u in range(n_sub):
            st_scr[hc, u] = st[:, u * ROUTE_LANES:(u + 1) * ROUTE_LANES]

    key_id = lax.broadcasted_iota(jnp.int32, (N_KEYS, ROUTE_LANES), 0).astype(F32)

    def side_ranks(blocks, exact):
        states = [{"rank": jnp.full((N_KEYS, ROUTE_LANES), 99.0, F32), "tops": []} for _ in blocks]

        def recorder(state):
            def on_take(t, m, taken):
                state["rank"] = jnp.where(taken, float(t), state["rank"])
                state["tops"].append(m)
            return on_take

        _take_top([(st_scr[hc, u], recorder(st)) for (hc, u), st in zip(blocks, states)], key_id, TOPK, exact)
        for (hc, u), st in zip(blocks, states):
            rk_scr[hc, u] = st["rank"]
            tv_scr[hc, u] = jnp.concatenate(st["tops"], axis=0)
        return [jnp.sum(jnp.where(st["rank"] < 99.0, 1.0, 0.0), axis=0, keepdims=True) for st in states]

    def side_topk(i, carry):
        blocks = [(i * SIDES_PER_PASS + d, u) for d in range(SIDES_PER_PASS) for u in range(n_sub)]
        counts = side_ranks(blocks, exact=False)

        @pl.when(_tied(jnp.concatenate(counts, axis=0)))
        def _():
            side_ranks(blocks, exact=True)
        return carry

    lax.fori_loop(0, 2 * PEER_H // SIDES_PER_PASS, side_topk, 0)

    flat, ok = flat_ref[...], ok_ref[...] > 0.0
    row16 = lax.broadcasted_iota(jnp.int32, (TOPK, ROUTE_LANES), 0)

    def joint_topk(i, carry):
        blocks = [(i * HEADS_PER_PASS + d, u) for d in range(HEADS_PER_PASS) for u in range(n_sub)]

        def candidates(h, u):
            t1, t2 = tv_scr[2 * h, u], tv_scr[2 * h + 1, u]
            slabs = []
            for kind, fixed, n in _SLABS:
                slabs.append(t1[fixed:fixed + 1] + t2[0:n] if kind == "a" else t1[0:n] + t2[fixed:fixed + 1])
            return jnp.where(ok, jnp.concatenate(slabs, axis=0), -jnp.inf)

        def select(exact):
            states = [{"sel": jnp.zeros((N_CAND, ROUTE_LANES), F32)} for _ in blocks]

            def recorder(state):
                def on_take(t, m, taken):
                    state["sel"] = jnp.where(taken, 1.0, state["sel"])
                return on_take

            _take_top([(candidates(h, u), recorder(st)) for (h, u), st in zip(blocks, states)], flat, TOPK, exact)
            for k, st in enumerate(states):
                sel_scr[k] = st["sel"]
            return [jnp.sum(st["sel"], axis=0, keepdims=True) for st in states]

        counts = select(exact=False)

        @pl.when(_tied(jnp.concatenate(counts, axis=0)))
        def _():
            select(exact=True)

        for k, (h, u) in enumerate(blocks):
            t1, t2 = tv_scr[2 * h, u], tv_scr[2 * h + 1, u]
            cand, sel = candidates(h, u), sel_scr[k]
            z = jnp.sum(jnp.where(sel > 0.0, jnp.exp(cand - cand[0:1]), 0.0), axis=0, keepdims=True)
            rank1, rank2 = rk_scr[2 * h, u], rk_scr[2 * h + 1, u]
            slab, start = {}, 0
            for kind, fixed, n in _SLABS:
                slab[kind, fixed] = sel[start:start + n]
                start += n
            n_all = slab["b", 0] + jnp.concatenate([slab["b", 1], jnp.zeros((TOPK - 8, ROUTE_LANES), F32)], axis=0)
            for kind, fixed, _ in _SLABS:
                if kind == "a":
                    n_all = n_all + jnp.where(row16 == fixed, jnp.sum(slab[kind, fixed], axis=0, keepdims=True), 0.0)
            n1 = jnp.zeros((N_KEYS, ROUTE_LANES), F32)
            for a in range(TOPK):
                n1 = jnp.where(rank1 == float(a), n_all[a:a + 1], n1)
            lanes = slice(u * ROUTE_LANES, (u + 1) * ROUTE_LANES)
            r2_ref[h, 0, :, lanes] = rank2.astype(r2_ref.dtype)
            n1_ref[h, 0, :, lanes] = n1
            e1_ref[h, 0, :, lanes] = jnp.exp(st_scr[2 * h, u] - t1[0:1]) / z
            e2_ref[h, 0, :, lanes] = jnp.exp(st_scr[2 * h + 1, u] - t2[0:1]).astype(e2_ref.dtype)
        return carry

    lax.fori_loop(0, PEER_H // HEADS_PER_PASS, joint_topk, 0)


def _route(layer, grp, x, g, mod, wq, keys, tables):
    tm = ROUTE_TM
    n_sub = tm // ROUTE_LANES
    n_tiles = grp.rows // tm
    table_spec = pl.BlockSpec((N_CAND, ROUTE_LANES), lambda i: (0, 0))
    route_spec = pl.BlockSpec((PEER_H, 1, N_KEYS, tm), lambda i: (0, i, 0, 0))
    tile_shape = jax.ShapeDtypeStruct((PEER_H, n_tiles, N_KEYS, tm), BF16)
    row_shape = jax.ShapeDtypeStruct((PEER_H, n_tiles, N_KEYS, tm), F32)
    return pl.pallas_call(
        _route_body,
        grid=(n_tiles,),
        in_specs=[
            pl.BlockSpec((tm, D), lambda i: (i, 0)),
            _layer_spec(layer, (1, D), lambda i: (0, 0)),
            _mod_spec(grp, tm, 4),
            _mod_spec(grp, tm, 3),
            _layer_spec(layer, (D, D), lambda i: (0, 0)),
            _layer_spec(layer, (2 * PEER_H, N_KEYS, N_KEYS), lambda i: (0, 0, 0)),
            table_spec, table_spec,
        ],
        out_specs=[pl.BlockSpec((D, tm), lambda i: (0, i)), route_spec, route_spec, route_spec, route_spec],
        out_shape=[jax.ShapeDtypeStruct((D, grp.rows), BF16), tile_shape, tile_shape, row_shape, row_shape],
        scratch_shapes=[
            pltpu.VMEM((2 * PEER_H, n_sub, N_KEYS, ROUTE_LANES), F32),
            pltpu.VMEM((2 * PEER_H, n_sub, N_KEYS, ROUTE_LANES), F32),
            pltpu.VMEM((2 * PEER_H, n_sub, TOPK, ROUTE_LANES), F32),
            pltpu.VMEM((HEADS_PER_PASS * n_sub, N_CAND, ROUTE_LANES), F32),
        ],
        compiler_params=_params(1, 48),
        name="peer_route",
    )(x, g, mod, mod, wq, keys, *tables)


PEER_TT = 1024
PEER_CH = ROUTE_TM
PEER_ET = 512


def _peer_gates(r2_ref, e2_ref, n1_ref, e1_ref, g_scr, slot, step):
    n_blk = PEER_ET // N_KEYS
    for ch in range(PEER_TT // PEER_CH):
        for ib in range(n_blk):
            i1 = step * n_blk + ib
            gate = jnp.zeros((N_KEYS, PEER_CH), BF16)
            for h in range(PEER_H):
                keep = r2_ref[h, ch] < n1_ref[h, ch, pl.ds(i1, 1), :].astype(BF16)
                weight = e2_ref[h, ch] * e1_ref[h, ch, pl.ds(i1, 1), :].astype(BF16)
                gate = gate + jnp.where(keep, weight, jnp.zeros_like(weight))
            g_scr[slot, ch, ib * N_KEYS:(ib + 1) * N_KEYS, :] = gate


def _peer_body(h_ref, u_ref, v_ref, r2_ref, e2_ref, n1_ref, e1_ref, o_ref, g_scr):
    j = pl.program_id(1)
    last = pl.num_programs(1) - 1
    slot = j % 2
    tables = (r2_ref, e2_ref, n1_ref, e1_ref)

    @pl.when(j == 0)
    def _():
        o_ref[...] = jnp.zeros_like(o_ref)
        _peer_gates(*tables, g_scr, 0, 0)

    _peer_gates(*tables, g_scr, 1 - slot, jnp.minimum(j + 1, last))

    u, v = u_ref[...], v_ref[...]
    n_ch = PEER_TT // PEER_CH
    quarter_e, half_d = PEER_ET // 4, D // 2
    chunk = lambda ch: slice(ch * PEER_CH, (ch + 1) * PEER_CH)
    acts = [[jnp.dot(u[s * quarter_e:(s + 1) * quarter_e], h_ref[:, chunk(ch)], preferred_element_type=F32)
             for s in range(4)] for ch in range(n_ch)]
    ps = []
    for ch in range(n_ch):
        parts = []
        for s in range(4):
            act = acts[ch][s]
            gelu = 0.5 * act * (1.0 + lax.erf(act * (2.0 ** -0.5)))
            parts.append(g_scr[slot, ch, s * quarter_e:(s + 1) * quarter_e, :] * gelu.astype(BF16))
        ps.append(jnp.concatenate(parts, axis=0))
    for ch in range(n_ch):
        for s in range(2):
            rows = slice(s * half_d, (s + 1) * half_d)
            o_ref[rows, chunk(ch)] += lax.dot_general(v[:, rows], ps[ch], TN_DIMS, preferred_element_type=F32)


def _peer(layer, grp, h2t, routing, u, v):
    tt, et = PEER_TT, PEER_ET
    n_ch = tt // PEER_CH
    once = pl.Buffered(1)
    route_spec = pl.BlockSpec((PEER_H, n_ch, N_KEYS, PEER_CH), lambda i, j: (0, i, 0, 0), pipeline_mode=once)
    return pl.pallas_call(
        _peer_body,
        grid=(grp.rows // tt, N_EXPERTS // et),
        in_specs=[
            pl.BlockSpec((D, tt), lambda i, j: (0, i), pipeline_mode=once),
            _layer_spec(layer, (et, D), lambda i, j: (j, 0)),
            _layer_spec(layer, (et, D), lambda i, j: (j, 0)),
            route_spec, route_spec, route_spec, route_spec,
        ],
        out_specs=pl.BlockSpec((D, tt), lambda i, j: (0, i)),
        out_shape=jax.ShapeDtypeStruct((D, grp.rows), F32),
        scratch_shapes=[pltpu.VMEM((2, n_ch, et, PEER_CH), BF16)],
        compiler_params=_params(2, 56),
        name="peer_dense",
    )(h2t, u, v, *routing)


def _peer_residual_body(f_ref, x_ref, gt_ref, fg_ref, o_ref, *, final_norm):
    x2 = x_ref[...] + gt_ref[...] * f_ref[...].T
    if final_norm:
        x2 = x2 * lax.rsqrt(jnp.mean(x2 * x2, axis=-1, keepdims=True) + EPS) * fg_ref[...]
    o_ref[...] = x2


def _peer_residual(grp, ffn_t, x, mod, final_g, final_norm):
    tm = 256
    return pl.pallas_call(
        functools.partial(_peer_residual_body, final_norm=final_norm),
        grid=(grp.rows // tm,),
        in_specs=[
            pl.BlockSpec((D, tm), lambda i: (0, i)),
            pl.BlockSpec((tm, D), lambda i: (i, 0)),
            _mod_spec(grp, tm, 5),
            pl.BlockSpec((1, D), lambda i: (0, 0)),
        ],
        out_specs=pl.BlockSpec((tm, D), lambda i: (i, 0)),
        out_shape=jax.ShapeDtypeStruct((grp.rows, D), F32),
        compiler_params=_params(1, 32),
        name="peer_residual",
    )(ffn_t, x, mod, final_g)


def _proj_weights(w_in):
    gla_end, glr_end = 6144, 6160
    sq, sk, sv, ga, gb = (w_in[..., a:b] for a, b in ((6160, 8208), (8208, 8464), (8464, 8720), (8720, 10768),
                                                       (10768, 12816)))
    pad = jnp.zeros(w_in.shape[:-1] + (N_PROJ - COL_GLR - GATE_RANK,), w_in.dtype)
    return jnp.concatenate([w_in[..., :gla_end], sq, ga, gb, sk, sv, w_in[..., gla_end:glr_end], pad],
                           axis=-1).astype(BF16)


def kernel(x_prompt, x_sample, state_gla, cache_swa_k, cache_swa_v, c_prompt, c_sample, w_ada, b_ada, norm1_g,
           norm2_g, w_in, w_alpha2, b_alpha, gla_norm_g, swa_sinks, w_out, peer_wq, peer_keys, peer_u, peer_v,
           final_g):
    n_p, l_p = x_prompt.shape[0], x_prompt.shape[1]
    n_s, l_s = x_sample.shape[0], x_sample.shape[1]
    prompt = _Group(n_p, l_p, l_p)
    sample = _Group(n_s, SAMPLE_SEQ_ROWS, l_s)
    kvw = SWA_KV * SWA_HD
    xs = {
        prompt: x_prompt.reshape(prompt.rows, D),
        sample: jnp.pad(x_sample, ((0, 0), (0, sample.seq_rows - l_s), (0, 0))).reshape(sample.rows, D),
    }
    c_all = jnp.concatenate([c_prompt, c_sample], axis=0)
    tables = tuple(jnp.asarray(t) for t in _candidate_tables())
    fg = final_g.reshape(1, D)

    row = lambda a: a.reshape(DEPTH, 1, -1)
    b_ada_r, g1, g2, ba, gn = row(b_ada), row(norm1_g), row(norm2_g), row(b_alpha), row(gla_norm_g)
    w_cat = _proj_weights(w_in)
    w2, w_o, wq = w_alpha2.astype(BF16), w_out.astype(BF16), peer_wq.astype(BF16)
    keys = peer_keys.reshape(DEPTH, 2 * PEER_H, N_KEYS, N_KEYS).astype(BF16)
    u, v = peer_u.astype(BF16), peer_v.astype(BF16)
    cache_k = cache_swa_k.reshape(DEPTH, n_s, WINDOW, kvw)
    cache_v = cache_swa_v.reshape(DEPTH, n_s, WINDOW, kvw)

    states = {prompt: None, sample: None}
    k_bufs, v_bufs = {prompt: [], sample: []}, {prompt: [], sample: []}
    for l in range(DEPTH):
        mod = _ada_mod(l, c_all, w_ada, b_ada_r)
        mods = {
            prompt: mod[:n_p].reshape(n_p, 1, 6 * D),
            sample: jnp.repeat(mod[n_p:], sample.seq_rows, axis=0).reshape(1, sample.rows, 6 * D),
        }
        for grp in (prompt, sample):
            x, m = xs[grp], mods[grp]
            proj = _norm_matmul(l, grp, x, g1, m, w_cat)
            heads = lambda a: a.reshape(grp.n_seq, grp.seq_rows, SWA_KV, SWA_HD)
            k_new, v_new = heads(proj[:, COL_SK:COL_SK + kvw]), heads(proj[:, COL_SV:COL_SV + kvw])
            if grp.dense:
                og, states[grp] = _gla(l, grp, proj, w2, ba, gn, None, states[grp], BF16)
                os_ = _swa_prompt(l, grp, proj, swa_sinks)
                k_buf, v_buf = k_new[:, -WINDOW:], v_new[:, -WINDOW:]
            else:
                og, states[grp] = _gla(l, grp, proj, w2, ba, gn, state_gla, states[grp], F32)
                os_ = _swa_sample(l, grp, proj, cache_k, cache_v, swa_sinks)
                k_buf = jnp.concatenate([cache_swa_k[l][:, l_s:], k_new[:, :l_s]], axis=1)
                v_buf = jnp.concatenate([cache_swa_v[l][:, l_s:], v_new[:, :l_s]], axis=1)
            x1 = _merge_out(l, grp, og, os_, proj, x, m, w_o)
            h2t, *routing = _route(l, grp, x1, g2, m, wq, keys, tables)
            ffn_t = _peer(l, grp, h2t, routing, u, v)
            xs[grp] = _peer_residual(grp, ffn_t, x1, m, fg, final_norm=(l == DEPTH - 1))
            k_bufs[grp].append(k_buf)
            v_bufs[grp].append(v_buf)

    y_prompt = xs[prompt].reshape(x_prompt.shape)
    y_sample = xs[sample].reshape(n_s, sample.seq_rows, D)[:, :l_s]
    caches = lambda grp: (states[grp], jnp.stack(k_bufs[grp]), jnp.stack(v_bufs[grp]))
    return (y_prompt, y_sample) + caches(prompt) + caches(sample)
```

```python
import functools
from typing import NamedTuple

import numpy as np
import jax
import jax.numpy as jnp
from jax import lax
from jax.experimental import pallas as pl
from jax.experimental.pallas import tpu as pltpu

F32 = jnp.float32
BF16 = jnp.bfloat16

D = 2048
DEPTH = 2
GLA_H, GLA_DK, GLA_DV, GATE_RANK, GATE_TEMP = 4, 256, 512, 16, 16.0
SWA_HD, SWA_KV, SWA_G, WINDOW = 64, 4, 8, 128
PEER_H, N_KEYS, TOPK = 8, 128, 16
N_EXPERTS = N_KEYS * N_KEYS
EPS = 1e-6

COL_GQ, COL_GK, COL_GV, COL_GR = 0, 1024, 2048, 4096
COL_SQ, COL_GA, COL_GB = 6144, 8192, 10240
COL_SK, COL_SV, COL_GLR = 12288, 12544, 12800
N_PROJ = 13312
GLR_BLOCK = 128

MIB = 1024 * 1024
NT_DIMS = (((1,), (1,)), ((), ()))
TN_DIMS = (((0,), (0,)), ((), ()))


class _Group(NamedTuple):
    n_seq: int
    seq_rows: int
    valid_rows: int

    @property
    def rows(self):
        return self.n_seq * self.seq_rows

    @property
    def dense(self):
        return self.valid_rows == self.seq_rows


SAMPLE_SEQ_ROWS = 8


def _params(n_grid, vmem_mib):
    return pltpu.CompilerParams(dimension_semantics=("arbitrary",) * n_grid, vmem_limit_bytes=vmem_mib * MIB)


def _mod_spec(grp, tm, comp, tn=D, with_j=False):
    per_comp = D // tn
    if grp.seq_rows >= tm:
        per_seq = grp.seq_rows // tm
        if with_j:
            return pl.BlockSpec((None, 1, tn), lambda i, j: (i // per_seq, 0, comp * per_comp + j))
        return pl.BlockSpec((None, 1, tn), lambda i, *_: (i // per_seq, 0, comp * per_comp))
    if with_j:
        return pl.BlockSpec((None, tm, tn), lambda i, j: (0, i, comp * per_comp + j))
    return pl.BlockSpec((None, tm, tn), lambda i, *_: (0, i, comp * per_comp))


def _layer_spec(layer, block, index_map):
    return pl.BlockSpec((None,) + tuple(block), lambda *grid: (layer,) + tuple(index_map(*grid)))


def _rms_modulate(x, g, sc, sh):
    xn = x * lax.rsqrt(jnp.mean(x * x, axis=-1, keepdims=True) + EPS)
    return xn * g * (1.0 + sc) + sh


def _silu(x):
    return x * jax.nn.sigmoid(x)


def _ada_body(c_ref, w_ref, b_ref, o_ref):
    a = _silu(c_ref[...]).astype(BF16)
    o_ref[...] = jnp.dot(a, w_ref[...].astype(BF16), preferred_element_type=F32) + b_ref[...]


def _ada_mod(layer, c_all, w_ada, b_ada):
    rows, tn = c_all.shape[0], 1024
    return pl.pallas_call(
        _ada_body,
        grid=(6 * D // tn,),
        in_specs=[
            pl.BlockSpec((rows, D), lambda j: (0, 0)),
            _layer_spec(layer, (D, tn), lambda j: (0, j)),
            _layer_spec(layer, (1, tn), lambda j: (0, j)),
        ],
        out_specs=pl.BlockSpec((rows, tn), lambda j: (0, j)),
        out_shape=jax.ShapeDtypeStruct((rows, 6 * D), F32),
        compiler_params=_params(1, 40),
        name="ada_mod",
    )(c_all, w_ada, b_ada)


def _norm_mm_body(x_ref, g_ref, sc_ref, sh_ref, w_ref, o_ref, h_scr):
    @pl.when(pl.program_id(1) == 0)
    def _():
        h_scr[...] = _rms_modulate(x_ref[...], g_ref[...], sc_ref[...], sh_ref[...]).astype(BF16)

    o_ref[...] = jnp.dot(h_scr[...], w_ref[...], preferred_element_type=F32)


def _norm_matmul(layer, grp, x, g, mod, w_cat):
    tm = 1024 if grp.dense else 512
    tn = 1024
    n = w_cat.shape[-1]
    return pl.pallas_call(
        _norm_mm_body,
        grid=(grp.rows // tm, n // tn),
        in_specs=[
            pl.BlockSpec((tm, D), lambda i, j: (i, 0)),
            _layer_spec(layer, (1, D), lambda i, j: (0, 0)),
            _mod_spec(grp, tm, 1),
            _mod_spec(grp, tm, 0),
            _layer_spec(layer, (D, tn), lambda i, j: (0, j)),
        ],
        out_specs=pl.BlockSpec((tm, tn), lambda i, j: (i, j)),
        out_shape=jax.ShapeDtypeStruct((grp.rows, n), F32),
        scratch_shapes=[pltpu.VMEM((tm, D), BF16)],
        compiler_params=_params(2, 48),
        name="norm_proj",
    )(x, g, mod, mod, w_cat)


def _split3(x):
    hi = x.astype(BF16).astype(F32)
    r1 = x - hi
    mid = r1.astype(BF16).astype(F32)
    lo = (r1 - mid).astype(BF16).astype(F32)
    return hi, mid, lo


def _gla_body(qk_ref, v_ref, gr_ref, glr_ref, w2_ref, ba_ref, gn_ref, *rest, layer, chunk, sub, valid, n_chunks,
              has_state, chained):
    s0_ref = rest[0] if has_state else None
    og_ref, so_ref, s_scr = rest[int(has_state) + int(chained):]
    c = chunk
    n = pl.program_id(1)

    @pl.when(n == 0)
    def _():
        s_scr[...] = s0_ref[...] if has_state else jnp.zeros_like(s_scr)

    row = lax.broadcasted_iota(jnp.int32, (c, 1), 0)
    live = row < valid
    tri = (lax.broadcasted_iota(jnp.int32, (c, c), 1) <= lax.broadcasted_iota(jnp.int32, (c, c), 0)).astype(BF16)
    lane = lax.broadcasted_iota(jnp.int32, (sub, c), 1)
    rowi = lax.broadcasted_iota(jnp.int32, (sub, c), 0)
    glr = glr_ref[:, 0:GATE_RANK].astype(BF16)
    ones8 = jnp.ones((8, GLA_DV), BF16)

    heads = range(GLA_H)
    dk = lambda hh: slice(hh * GLA_DK, (hh + 1) * GLA_DK)
    dv = lambda hh: slice(hh * GLA_DV, (hh + 1) * GLA_DV)
    qs = [qk_ref[:, COL_GQ + hh * GLA_DK:COL_GQ + (hh + 1) * GLA_DK] * (GLA_DK ** -0.5) for hh in heads]
    ks = [qk_ref[:, COL_GK + hh * GLA_DK:COL_GK + (hh + 1) * GLA_DK] for hh in heads]
    vs = [v_ref[:, dv(hh)] for hh in heads]
    zs = [jnp.dot(glr, w2_ref[:, dk(hh)], preferred_element_type=F32) + ba_ref[:, dk(hh)] for hh in heads]
    gs = [(jnp.minimum(z, 0.0) - jnp.log(1.0 + jnp.exp(-jnp.abs(z)))) * (1.0 / GATE_TEMP) for z in zs]
    if valid < c:
        gs = [jnp.where(live, g, 0.0) for g in gs]
        ks = [jnp.where(live, k, 0.0) for k in ks]
    bs = [sum(jnp.dot(tri, p.astype(BF16), preferred_element_type=F32) for p in _split3(g)) for g in gs]
    vbs = [v.astype(BF16) for v in vs]

    o_intras = [jnp.zeros((c, GLA_DV), F32) for _ in heads]
    p_rows = [[] for _ in heads]
    for i in range(c // sub):
        r0 = i * sub
        for hh in heads:
            b, q, k, v = bs[hh], qs[hh], ks[hh], vs[hh]
            b_i, q_i, k_i = b[r0:r0 + sub], q[r0:r0 + sub], k[r0:r0 + sub]
            att = jnp.zeros((sub, c), F32)
            for s in range(min(sub, valid - r0)):
                e = jnp.exp(jnp.minimum(b_i - b_i[s:s + 1], 0.0))
                col = jnp.sum(q_i * k_i[s:s + 1] * e, axis=-1, keepdims=True)
                if c == sub:
                    o_intras[hh] = o_intras[hh] + jnp.where(row >= s, col, 0.0) * v[s:s + 1]
                else:
                    att = jnp.where(lane == r0 + s, col, att)
            if c == sub:
                continue
            att = jnp.where(lane <= rowi + r0, att, 0.0)
            if i > 0:
                ref = b[r0 - 1:r0]
                qt = (q_i * jnp.exp(b_i - ref)).astype(BF16)
                kt = (k * jnp.exp(jnp.minimum(ref - b, 0.0))).astype(BF16)
                off = lax.dot_general(qt, kt, NT_DIMS, preferred_element_type=F32)
                att = jnp.where(lane < r0, off, att)
            p_rows[hh].append(att)
    if c != sub:
        o_intras = [jnp.dot(jnp.concatenate(p_rows[hh], axis=0).astype(BF16), vbs[hh], preferred_element_type=F32)
                    for hh in heads]

    s_olds = [s_scr[hh] for hh in heads]
    outs = [jnp.dot((qs[hh] * jnp.exp(bs[hh])).astype(BF16), s_olds[hh].astype(BF16), preferred_element_type=F32)
            + o_intras[hh] for hh in heads]
    for hh in heads:
        o = outs[hh]
        on = o * lax.rsqrt(jnp.mean(o * o, axis=-1, keepdims=True) + EPS) * gn_ref[...]
        og_ref[:, dv(hh)] = (on * _silu(gr_ref[:, dv(hh)])).astype(og_ref.dtype)

    b_ends = [b[c - 1:c] for b in bs]
    khats = [(ks[hh] * jnp.exp(b_ends[hh] - bs[hh])).astype(BF16) for hh in heads]
    d_ss = [lax.dot_general(khats[hh], vbs[hh], TN_DIMS, preferred_element_type=F32) for hh in heads]
    decays = []
    for hh in heads:
        pieces = jnp.concatenate(list(_split3(jnp.exp(b_ends[hh]))) + [jnp.zeros((5, GLA_DK), F32)], axis=0)
        decays.append(lax.dot_general(pieces.astype(BF16), ones8, TN_DIMS, preferred_element_type=F32))
    for hh in heads:
        s_scr[hh] = s_olds[hh] * decays[hh] + d_ss[hh]

    @pl.when(n == n_chunks - 1)
    def _():
        if chained:
            so_ref[...] = s_scr[...]
        else:
            for l in range(DEPTH):
                so_ref[l] = s_scr[...] if l == layer else jnp.zeros_like(s_scr)


def _gla(layer, grp, proj, w2, ba, gn, s0, states, out_dtype):
    chunk, sub = (64, 8) if grp.dense else (grp.seq_rows, grp.seq_rows)
    nc = grp.seq_rows // chunk
    has_state, chained = s0 is not None, states is not None
    body = functools.partial(_gla_body, layer=layer, chunk=chunk, sub=sub, valid=min(chunk, grp.valid_rows),
                             n_chunks=nc, has_state=has_state, chained=chained)
    state_spec = _layer_spec(layer, (None, GLA_H, GLA_DK, GLA_DV), lambda b, n: (b, 0, 0, 0))
    all_layers_spec = pl.BlockSpec((DEPTH, None, GLA_H, GLA_DK, GLA_DV), lambda b, n: (0, b, 0, 0, 0))
    in_specs = [
        pl.BlockSpec((chunk, 2048), lambda b, n: (b * nc + n, 0)),
        pl.BlockSpec((chunk, 2048), lambda b, n: (b * nc + n, COL_GV // 2048)),
        pl.BlockSpec((chunk, 2048), lambda b, n: (b * nc + n, COL_GR // 2048)),
        pl.BlockSpec((chunk, GLR_BLOCK), lambda b, n: (b * nc + n, COL_GLR // GLR_BLOCK)),
        _layer_spec(layer, (GATE_RANK, GLA_H * GLA_DK), lambda b, n: (0, 0)),
        _layer_spec(layer, (1, GLA_H * GLA_DK), lambda b, n: (0, 0)),
        _layer_spec(layer, (1, GLA_DV), lambda b, n: (0, 0)),
    ]
    args = [proj, proj, proj, proj, w2, ba, gn]
    if has_state:
        in_specs.append(state_spec)
        args.append(s0)
    if chained:
        in_specs.append(pl.BlockSpec(memory_space=pl.ANY))
        args.append(states)
    return pl.pallas_call(
        body,
        grid=(grp.n_seq, nc),
        in_specs=in_specs,
        out_specs=[pl.BlockSpec((chunk, 2048), lambda b, n: (b * nc + n, 0)),
                   state_spec if chained else all_layers_spec],
        out_shape=[
            jax.ShapeDtypeStruct((grp.rows, 2048), out_dtype),
            jax.ShapeDtypeStruct((DEPTH, grp.n_seq, GLA_H, GLA_DK, GLA_DV), F32),
        ],
        input_output_aliases={len(args) - 1: 1} if chained else {},
        scratch_shapes=[pltpu.VMEM((GLA_H, GLA_DK, GLA_DV), F32)],
        compiler_params=_params(2, 40),
        name="gla",
    )(*args)


def _sink_attention(problems, bias):
    sts = [lax.dot_general(k, q, NT_DIMS, preferred_element_type=F32) + bias for q, k, _, _ in problems]
    ms = [jnp.maximum(jnp.max(st, axis=0, keepdims=True), sink) for st, (_, _, _, sink) in zip(sts, problems)]
    ps = [jnp.exp(st - m) for st, m in zip(sts, ms)]
    dens = [jnp.sum(p, axis=0, keepdims=True) + jnp.exp(sink - m) for p, m, (_, _, _, sink) in zip(ps, ms, problems)]
    ots = [lax.dot_general(v, p.astype(BF16), TN_DIMS, preferred_element_type=F32)
           for p, (_, _, v, _) in zip(ps, problems)]
    return [(ot / den).T for ot, den in zip(ots, dens)]


def _kv_head(parts, kv, scale=None):
    cols = slice(kv * SWA_HD, (kv + 1) * SWA_HD)
    x = jnp.concatenate([r if r.shape[1] == SWA_HD else r[:, cols] for r in parts], axis=0)
    return (x if scale is None else x * scale).astype(BF16)


def _swa_prompt_body(sinks_ref, q_ref, kp_ref, kc_ref, vp_ref, vc_ref, o_ref, *, layer):
    i = pl.program_id(1)
    w = WINDOW
    kpos = lax.broadcasted_iota(jnp.int32, (2 * w, w), 0)
    dist = lax.broadcasted_iota(jnp.int32, (2 * w, w), 1) + w - kpos
    bias = jnp.where((dist >= 0) & (dist < w) & ((kpos >= w) | (i > 0)), 0.0, -jnp.inf)
    for kv in range(SWA_KV):
        k2 = _kv_head([kp_ref, kc_ref], kv, SWA_HD ** -0.5)
        v2 = _kv_head([vp_ref, vc_ref], kv)
        heads = range(kv * SWA_G, (kv + 1) * SWA_G)
        problems = [(q_ref[:, h * SWA_HD:(h + 1) * SWA_HD].astype(BF16), k2, v2, sinks_ref[layer, h]) for h in heads]
        for h, o in zip(heads, _sink_attention(problems, bias)):
            o_ref[:, h * SWA_HD:(h + 1) * SWA_HD] = o.astype(o_ref.dtype)


def _swa_prompt(layer, grp, proj, sinks):
    w = WINDOW
    nb = grp.seq_rows // w
    kvw = SWA_KV * SWA_HD
    cur = lambda col: (lambda b, i: (b * nb + i, col))
    prev = lambda col: (lambda b, i: (b * nb + jnp.maximum(i - 1, 0), col))
    return pl.pallas_call(
        functools.partial(_swa_prompt_body, layer=layer),
        grid=(grp.n_seq, nb),
        in_specs=[
            pl.BlockSpec(memory_space=pltpu.SMEM),
            pl.BlockSpec((w, 2048), cur(COL_SQ // 2048)),
            pl.BlockSpec((w, kvw), prev(COL_SK // kvw)),
            pl.BlockSpec((w, kvw), cur(COL_SK // kvw)),
            pl.BlockSpec((w, kvw), prev(COL_SV // kvw)),
            pl.BlockSpec((w, kvw), cur(COL_SV // kvw)),
        ],
        out_specs=pl.BlockSpec((w, 2048), lambda b, i: (b * nb + i, 0)),
        out_shape=jax.ShapeDtypeStruct((grp.rows, 2048), BF16),
        compiler_params=_params(2, 40),
        name="swa_prompt",
    )(sinks, proj, proj, proj, proj, proj)


SWA_SAMPLE_SEQS = 4


def _swa_sample_body(sinks_ref, q_ref, kn_ref, vn_ref, ck_ref, cv_ref, o_ref, *, layer, rows, valid):
    w = WINDOW
    shape = (2 * w, SWA_G * rows)
    qi = lax.broadcasted_iota(jnp.int32, shape, 1) & (rows - 1)
    key = lax.broadcasted_iota(jnp.int32, shape, 0)
    visible = ((key < w) & (key > qi)) | ((key >= w) & (key - w <= qi) & (key < w + valid))
    bias = jnp.where(visible, 0.0, -jnp.inf)
    pad = jnp.zeros((w - rows, SWA_HD), F32)
    sinks = [jnp.concatenate([jnp.full((1, rows), sinks_ref[layer, kv * SWA_G + g], F32) for g in range(SWA_G)],
                             axis=1) for kv in range(SWA_KV)]
    problems = []
    for s in range(SWA_SAMPLE_SEQS):
        seq = slice(s * rows, (s + 1) * rows)
        for kv in range(SWA_KV):
            cols = slice(kv * SWA_HD, (kv + 1) * SWA_HD)
            k2 = jnp.concatenate([ck_ref[s, :, cols], kn_ref[seq, cols], pad], axis=0) * (SWA_HD ** -0.5)
            v2 = jnp.concatenate([cv_ref[s, :, cols], vn_ref[seq, cols], pad], axis=0)
            heads = range(kv * SWA_G, (kv + 1) * SWA_G)
            q = jnp.concatenate([q_ref[seq, h * SWA_HD:(h + 1) * SWA_HD] for h in heads], axis=0)
            problems.append((q.astype(BF16), k2.astype(BF16), v2.astype(BF16), sinks[kv]))
    for idx, o in enumerate(_sink_attention(problems, bias)):
        s, kv = divmod(idx, SWA_KV)
        for g in range(SWA_G):
            h = kv * SWA_G + g
            o_ref[s * rows:(s + 1) * rows, h * SWA_HD:(h + 1) * SWA_HD] = o[g * rows:(g + 1) * rows].astype(o_ref.dtype)


def _swa_sample(layer, grp, proj, cache_k, cache_v, sinks):
    w = WINDOW
    n = SWA_SAMPLE_SEQS
    rows = grp.seq_rows
    kvw = SWA_KV * SWA_HD
    body = functools.partial(_swa_sample_body, layer=layer, rows=rows, valid=grp.valid_rows)
    cache_spec = _layer_spec(layer, (n, w, kvw), lambda b: (b, 0, 0))
    return pl.pallas_call(
        body,
        grid=(grp.n_seq // n,),
        in_specs=[
            pl.BlockSpec(memory_space=pltpu.SMEM),
            pl.BlockSpec((n * rows, 2048), lambda b: (b, COL_SQ // 2048)),
            pl.BlockSpec((n * rows, kvw), lambda b: (b, COL_SK // kvw)),
            pl.BlockSpec((n * rows, kvw), lambda b: (b, COL_SV // kvw)),
            cache_spec,
            cache_spec,
        ],
        out_specs=pl.BlockSpec((n * rows, 2048), lambda b: (b, 0)),
        out_shape=jax.ShapeDtypeStruct((grp.rows, 2048), F32),
        compiler_params=_params(1, 32),
        name="swa_sample",
    )(sinks, proj, proj, proj, cache_k, cache_v)


def _merge_body(og_ref, os_ref, ga_ref, gb_ref, x_ref, gt_ref, w_ref, o_ref):
    merged = (jax.nn.sigmoid(ga_ref[...]) * og_ref[...].astype(F32)
              + jax.nn.sigmoid(gb_ref[...]) * os_ref[...].astype(F32)).astype(BF16)
    half = D // 2
    ys = [jnp.dot(merged, w_ref[:, s * half:(s + 1) * half], preferred_element_type=F32) for s in range(2)]
    for s, y in enumerate(ys):
        cols = slice(s * half, (s + 1) * half)
        o_ref[:, cols] = x_ref[:, cols] + gt_ref[:, cols] * y


def _merge_out(layer, grp, og, os_, proj, x, mod, w_out):
    tm = 512 if grp.dense else 256
    rows = lambda i: (i, 0)
    return pl.pallas_call(
        _merge_body,
        grid=(grp.rows // tm,),
        in_specs=[
            pl.BlockSpec((tm, D), rows),
            pl.BlockSpec((tm, D), rows),
            pl.BlockSpec((tm, D), lambda i: (i, COL_GA // D)),
            pl.BlockSpec((tm, D), lambda i: (i, COL_GB // D)),
            pl.BlockSpec((tm, D), rows),
            _mod_spec(grp, tm, 2),
            pl.BlockSpec((None, D, D), lambda i: (layer, 0, 0), pipeline_mode=pl.Buffered(1)),
        ],
        out_specs=pl.BlockSpec((tm, D), rows),
        out_shape=jax.ShapeDtypeStruct((grp.rows, D), F32),
        compiler_params=_params(1, 54),
        name="merge_out",
    )(og, os_, proj, proj, x, mod, w_out)


ROUTE_TM = 256
ROUTE_LANES = 128
SIDES_PER_PASS = 2
HEADS_PER_PASS = 2

_SLABS = (("a", 0, 16), ("a", 1, 8), ("b", 0, 16), ("b", 1, 8), ("a", 2, 8), ("a", 3, 8), ("a", 4, 8))


def _candidate_tables():
    flat, ok, seen = [], [], set()
    for kind, fixed, n in _SLABS:
        for r in range(n):
            a, b = (fixed, r) if kind == "a" else (r, fixed)
            good = (a + 1) * (b + 1) <= TOPK and (a, b) not in seen
            if good:
                seen.add((a, b))
            flat.append(a * TOPK + b if good else 1e9)
            ok.append(1.0 if good else 0.0)
    assert len(seen) == 50
    tile = lambda v: np.tile(np.asarray(v, np.float32)[:, None], (1, ROUTE_LANES))
    return tile(flat), tile(ok)


N_CAND = sum(n for _, _, n in _SLABS)


def _take_top(chains, tie_key, n_take, exact):
    vals = [v for v, _ in chains]
    for t in range(n_take):
        for i, (_, on_take) in enumerate(chains):
            m = jnp.max(vals[i], axis=0, keepdims=True)
            taken = vals[i] == m
            if exact:
                first = jnp.min(jnp.where(taken, tie_key, 2e9), axis=0, keepdims=True)
                taken = tie_key == first
            on_take(t, m, taken)
            vals[i] = jnp.where(taken, -jnp.inf, vals[i])


def _tied(taken_count):
    return jnp.max(taken_count) > float(TOPK)


def _route_body(x_ref, g_ref, sc_ref, sh_ref, wq_ref, keys_ref, flat_ref, ok_ref,
                h_ref, r2_ref, e2_ref, n1_ref, e1_ref, st_scr, rk_scr, tv_scr, sel_scr):
    h = _rms_modulate(x_ref[...], g_ref[...], sc_ref[...], sh_ref[...])
    h_ref[...] = h.T.astype(BF16)
    qp = jnp.dot(h.astype(BF16), wq_ref[...], preferred_element_type=F32)
    n_sub = ROUTE_TM // ROUTE_LANES
    for hc in range(2 * PEER_H):
        st = lax.dot_general(keys_ref[hc], qp[:, hc * N_KEYS:(hc + 1) * N_KEYS].astype(BF16), NT_DIMS,
                             preferred_element_type=F32)
        for u in range(n_sub):
            st_scr[hc, u] = st[:, u * ROUTE_LANES:(u + 1) * ROUTE_LANES]

    key_id = lax.broadcasted_iota(jnp.int32, (N_KEYS, ROUTE_LANES), 0).astype(F32)

    def side_ranks(blocks, exact):
        states = [{"rank": jnp.full((N_KEYS, ROUTE_LANES), 99.0, F32), "tops": []} for _ in blocks]

        def recorder(state):
            def on_take(t, m, taken):
                state["rank"] = jnp.where(taken, float(t), state["rank"])
                state["tops"].append(m)
            return on_take

        _take_top([(st_scr[hc, u], recorder(st)) for (hc, u), st in zip(blocks, states)], key_id, TOPK, exact)
        for (hc, u), st in zip(blocks, states):
            rk_scr[hc, u] = st["rank"]
            tv_scr[hc, u] = jnp.concatenate(st["tops"], axis=0)
        return [jnp.sum(jnp.where(st["rank"] < 99.0, 1.0, 0.0), axis=0, keepdims=True) for st in states]

    def side_topk(i, carry):
        blocks = [(i * SIDES_PER_PASS + d, u) for d in range(SIDES_PER_PASS) for u in range(n_sub)]
        counts = side_ranks(blocks, exact=False)

        @pl.when(_tied(jnp.concatenate(counts, axis=0)))
        def _():
            side_ranks(blocks, exact=True)
        return carry

    lax.fori_loop(0, 2 * PEER_H // SIDES_PER_PASS, side_topk, 0)

    flat, ok = flat_ref[...], ok_ref[...] > 0.0
    row16 = lax.broadcasted_iota(jnp.int32, (TOPK, ROUTE_LANES), 0)

    def joint_topk(i, carry):
        blocks = [(i * HEADS_PER_PASS + d, u) for d in range(HEADS_PER_PASS) for u in range(n_sub)]

        def candidates(h, u):
            t1, t2 = tv_scr[2 * h, u], tv_scr[2 * h + 1, u]
            slabs = []
            for kind, fixed, n in _SLABS:
                slabs.append(t1[fixed:fixed + 1] + t2[0:n] if kind == "a" else t1[0:n] + t2[fixed:fixed + 1])
            return jnp.where(ok, jnp.concatenate(slabs, axis=0), -jnp.inf)

        def select(exact):
            states = [{"sel": jnp.zeros((N_CAND, ROUTE_LANES), F32)} for _ in blocks]

            def recorder(state):
                def on_take(t, m, taken):
                    state["sel"] = jnp.where(taken, 1.0, state["sel"])
                return on_take

            _take_top([(candidates(h, u), recorder(st)) for (h, u), st in zip(blocks, states)], flat, TOPK, exact)
            for k, st in enumerate(states):
                sel_scr[k] = st["sel"]
            return [jnp.sum(st["sel"], axis=0, keepdims=True) for st in states]

        counts = select(exact=False)

        @pl.when(_tied(jnp.concatenate(counts, axis=0)))
        def _():
            select(exact=True)

        for k, (h, u) in enumerate(blocks):
            t1, t2 = tv_scr[2 * h, u], tv_scr[2 * h + 1, u]
            cand, sel = candidates(h, u), sel_scr[k]
            z = jnp.sum(jnp.where(sel > 0.0, jnp.exp(cand - cand[0:1]), 0.0), axis=0, keepdims=True)
            rank1, rank2 = rk_scr[2 * h, u], rk_scr[2 * h + 1, u]
            slab, start = {}, 0
            for kind, fixed, n in _SLABS:
                slab[kind, fixed] = sel[start:start + n]
                start += n
            n_all = slab["b", 0] + jnp.concatenate([slab["b", 1], jnp.zeros((TOPK - 8, ROUTE_LANES), F32)], axis=0)
            for kind, fixed, _ in _SLABS:
                if kind == "a":
                    n_all = n_all + jnp.where(row16 == fixed, jnp.sum(slab[kind, fixed], axis=0, keepdims=True), 0.0)
            n1 = jnp.zeros((N_KEYS, ROUTE_LANES), F32)
            for a in range(TOPK):
                n1 = jnp.where(rank1 == float(a), n_all[a:a + 1], n1)
            lanes = slice(u * ROUTE_LANES, (u + 1) * ROUTE_LANES)
            r2_ref[h, 0, :, lanes] = rank2.astype(r2_ref.dtype)
            n1_ref[h, 0, :, lanes] = n1
            e1_ref[h, 0, :, lanes] = jnp.exp(st_scr[2 * h, u] - t1[0:1]) / z
            e2_ref[h, 0, :, lanes] = jnp.exp(st_scr[2 * h + 1, u] - t2[0:1]).astype(e2_ref.dtype)
        return carry

    lax.fori_loop(0, PEER_H // HEADS_PER_PASS, joint_topk, 0)


def _route(layer, grp, x, g, mod, wq, keys, tables):
    tm = ROUTE_TM
    n_sub = tm // ROUTE_LANES
    n_tiles = grp.rows // tm
    table_spec = pl.BlockSpec((N_CAND, ROUTE_LANES), lambda i: (0, 0))
    route_spec = pl.BlockSpec((PEER_H, 1, N_KEYS, tm), lambda i: (0, i, 0, 0))
    tile_shape = jax.ShapeDtypeStruct((PEER_H, n_tiles, N_KEYS, tm), BF16)
    row_shape = jax.ShapeDtypeStruct((PEER_H, n_tiles, N_KEYS, tm), F32)
    return pl.pallas_call(
        _route_body,
        grid=(n_tiles,),
        in_specs=[
            pl.BlockSpec((tm, D), lambda i: (i, 0)),
            _layer_spec(layer, (1, D), lambda i: (0, 0)),
            _mod_spec(grp, tm, 4),
            _mod_spec(grp, tm, 3),
            _layer_spec(layer, (D, D), lambda i: (0, 0)),
            _layer_spec(layer, (2 * PEER_H, N_KEYS, N_KEYS), lambda i: (0, 0, 0)),
            table_spec, table_spec,
        ],
        out_specs=[pl.BlockSpec((D, tm), lambda i: (0, i)), route_spec, route_spec, route_spec, route_spec],
        out_shape=[jax.ShapeDtypeStruct((D, grp.rows), BF16), tile_shape, tile_shape, row_shape, row_shape],
        scratch_shapes=[
            pltpu.VMEM((2 * PEER_H, n_sub, N_KEYS, ROUTE_LANES), F32),
            pltpu.VMEM((2 * PEER_H, n_sub, N_KEYS, ROUTE_LANES), F32),
            pltpu.VMEM((2 * PEER_H, n_sub, TOPK, ROUTE_LANES), F32),
            pltpu.VMEM((HEADS_PER_PASS * n_sub, N_CAND, ROUTE_LANES), F32),
        ],
        compiler_params=_params(1, 48),
        name="peer_route",
    )(x, g, mod, mod, wq, keys, *tables)


PEER_TT = 1024
PEER_CH = ROUTE_TM
PEER_ET = 1024


def _peer_gates(r2_ref, e2_ref, n1_ref, e1_ref, g_scr, slot):
    for ch in range(PEER_TT // PEER_CH):
        for ib in range(PEER_ET // N_KEYS):
            gate = jnp.zeros((N_KEYS, PEER_CH), BF16)
            for h in range(PEER_H):
                keep = r2_ref[h, ch] < n1_ref[h, ch, ib:ib + 1, :].astype(BF16)
                weight = e2_ref[h, ch] * e1_ref[h, ch, ib:ib + 1, :].astype(BF16)
                gate = gate + jnp.where(keep, weight, jnp.zeros_like(weight))
            g_scr[slot, ch, ib * N_KEYS:(ib + 1) * N_KEYS, :] = gate


def _peer_body(h_ref, u_ref, v_ref, r2_ref, e2_ref, n1_first, e1_first, n1_next, e1_next, o_ref, g_scr):
    j = pl.program_id(1)
    slot = j % 2

    @pl.when(j == 0)
    def _():
        o_ref[...] = jnp.zeros_like(o_ref)
        _peer_gates(r2_ref, e2_ref, n1_first, e1_first, g_scr, 0)

    _peer_gates(r2_ref, e2_ref, n1_next, e1_next, g_scr, 1 - slot)

    u, v = u_ref[...], v_ref[...]
    n_ch = PEER_TT // PEER_CH
    quarter_e, half_d = PEER_ET // 4, D // 2
    chunk = lambda ch: slice(ch * PEER_CH, (ch + 1) * PEER_CH)
    acts = [[jnp.dot(u[s * quarter_e:(s + 1) * quarter_e], h_ref[:, chunk(ch)], preferred_element_type=F32)
             for s in range(4)] for ch in range(n_ch)]
    ps = []
    for ch in range(n_ch):
        parts = []
        for s in range(4):
            act = acts[ch][s]
            gelu = 0.5 * act * (1.0 + lax.erf(act * (2.0 ** -0.5)))
            parts.append(g_scr[slot, ch, s * quarter_e:(s + 1) * quarter_e, :] * gelu.astype(BF16))
        ps.append(jnp.concatenate(parts, axis=0))
    for ch in range(n_ch):
        for s in range(2):
            rows = slice(s * half_d, (s + 1) * half_d)
            o_ref[rows, chunk(ch)] += lax.dot_general(v[:, rows], ps[ch], TN_DIMS, preferred_element_type=F32)


def _peer(layer, grp, h2t, routing, u, v):
    tt, et = PEER_TT, PEER_ET
    n_ch = tt // PEER_CH
    once = pl.Buffered(1)
    route_spec = pl.BlockSpec((PEER_H, n_ch, N_KEYS, PEER_CH), lambda i, j: (0, i, 0, 0), pipeline_mode=once)
    n_blk, n_steps = et // N_KEYS, N_EXPERTS // et
    first_rows = pl.BlockSpec((PEER_H, n_ch, n_blk, PEER_CH), lambda i, j: (0, i, 0, 0))
    next_rows = pl.BlockSpec((PEER_H, n_ch, n_blk, PEER_CH), lambda i, j: (0, i, jnp.minimum(j + 1, n_steps - 1), 0))
    return pl.pallas_call(
        _peer_body,
        grid=(grp.rows // tt, N_EXPERTS // et),
        in_specs=[
            pl.BlockSpec((D, tt), lambda i, j: (0, i), pipeline_mode=once),
            _layer_spec(layer, (et, D), lambda i, j: (j, 0)),
            _layer_spec(layer, (et, D), lambda i, j: (j, 0)),
            route_spec, route_spec, first_rows, first_rows, next_rows, next_rows,
        ],
        out_specs=pl.BlockSpec((D, tt), lambda i, j: (0, i)),
        out_shape=jax.ShapeDtypeStruct((D, grp.rows), F32),
        scratch_shapes=[pltpu.VMEM((2, n_ch, et, PEER_CH), BF16)],
        compiler_params=_params(2, 56),
        name="peer_dense",
    )(h2t, u, v, routing[0], routing[1], routing[2], routing[3], routing[2], routing[3])


def _peer_residual_body(f_ref, x_ref, gt_ref, fg_ref, o_ref, *, final_norm):
    x2 = x_ref[...] + gt_ref[...] * f_ref[...].T
    if final_norm:
        x2 = x2 * lax.rsqrt(jnp.mean(x2 * x2, axis=-1, keepdims=True) + EPS) * fg_ref[...]
    o_ref[...] = x2


def _peer_residual(grp, ffn_t, x, mod, final_g, final_norm):
    tm = 256
    return pl.pallas_call(
        functools.partial(_peer_residual_body, final_norm=final_norm),
        grid=(grp.rows // tm,),
        in_specs=[
            pl.BlockSpec((D, tm), lambda i: (0, i)),
            pl.BlockSpec((tm, D), lambda i: (i, 0)),
            _mod_spec(grp, tm, 5),
            pl.BlockSpec((1, D), lambda i: (0, 0)),
        ],
        out_specs=pl.BlockSpec((tm, D), lambda i: (i, 0)),
        out_shape=jax.ShapeDtypeStruct((grp.rows, D), F32),
        compiler_params=_params(1, 32),
        name="peer_residual",
    )(ffn_t, x, mod, final_g)


def _proj_weights(w_in):
    gla_end, glr_end = 6144, 6160
    sq, sk, sv, ga, gb = (w_in[..., a:b] for a, b in ((6160, 8208), (8208, 8464), (8464, 8720), (8720, 10768),
                                                       (10768, 12816)))
    pad = jnp.zeros(w_in.shape[:-1] + (N_PROJ - COL_GLR - GATE_RANK,), w_in.dtype)
    return jnp.concatenate([w_in[..., :gla_end], sq, ga, gb, sk, sv, w_in[..., gla_end:glr_end], pad],
                           axis=-1).astype(BF16)


def kernel(x_prompt, x_sample, state_gla, cache_swa_k, cache_swa_v, c_prompt, c_sample, w_ada, b_ada, norm1_g,
           norm2_g, w_in, w_alpha2, b_alpha, gla_norm_g, swa_sinks, w_out, peer_wq, peer_keys, peer_u, peer_v,
           final_g):
    n_p, l_p = x_prompt.shape[0], x_prompt.shape[1]
    n_s, l_s = x_sample.shape[0], x_sample.shape[1]
    prompt = _Group(n_p, l_p, l_p)
    sample = _Group(n_s, SAMPLE_SEQ_ROWS, l_s)
    kvw = SWA_KV * SWA_HD
    xs = {
        prompt: x_prompt.reshape(prompt.rows, D),
        sample: jnp.pad(x_sample, ((0, 0), (0, sample.seq_rows - l_s), (0, 0))).reshape(sample.rows, D),
    }
    c_all = jnp.concatenate([c_prompt, c_sample], axis=0)
    tables = tuple(jnp.asarray(t) for t in _candidate_tables())
    fg = final_g.reshape(1, D)

    row = lambda a: a.reshape(DEPTH, 1, -1)
    b_ada_r, g1, g2, ba, gn = row(b_ada), row(norm1_g), row(norm2_g), row(b_alpha), row(gla_norm_g)
    w_cat = _proj_weights(w_in)
    w2, w_o, wq = w_alpha2.astype(BF16), w_out.astype(BF16), peer_wq.astype(BF16)
    keys = peer_keys.reshape(DEPTH, 2 * PEER_H, N_KEYS, N_KEYS).astype(BF16)
    u, v = peer_u.astype(BF16), peer_v.astype(BF16)
    cache_k = cache_swa_k.reshape(DEPTH, n_s, WINDOW, kvw)
    cache_v = cache_swa_v.reshape(DEPTH, n_s, WINDOW, kvw)

    states = {prompt: None, sample: None}
    k_bufs, v_bufs = {prompt: [], sample: []}, {prompt: [], sample: []}
    for l in range(DEPTH):
        mod = _ada_mod(l, c_all, w_ada, b_ada_r)
        mods = {
            prompt: mod[:n_p].reshape(n_p, 1, 6 * D),
            sample: jnp.repeat(mod[n_p:], sample.seq_rows, axis=0).reshape(1, sample.rows, 6 * D),
        }
        for grp in (prompt, sample):
            x, m = xs[grp], mods[grp]
            proj = _norm_matmul(l, grp, x, g1, m, w_cat)
            heads = lambda a: a.reshape(grp.n_seq, grp.seq_rows, SWA_KV, SWA_HD)
            k_new, v_new = heads(proj[:, COL_SK:COL_SK + kvw]), heads(proj[:, COL_SV:COL_SV + kvw])
            if grp.dense:
                og, states[grp] = _gla(l, grp, proj, w2, ba, gn, None, states[grp], BF16)
                os_ = _swa_prompt(l, grp, proj, swa_sinks)
                k_buf, v_buf = k_new[:, -WINDOW:], v_new[:, -WINDOW:]
            else:
                og, states[grp] = _gla(l, grp, proj, w2, ba, gn, state_gla, states[grp], F32)
                os_ = _swa_sample(l, grp, proj, cache_k, cache_v, swa_sinks)
                k_buf = jnp.concatenate([cache_swa_k[l][:, l_s:], k_new[:, :l_s]], axis=1)
                v_buf = jnp.concatenate([cache_swa_v[l][:, l_s:], v_new[:, :l_s]], axis=1)
            x1 = _merge_out(l, grp, og, os_, proj, x, m, w_o)
            h2t, *routing = _route(l, grp, x1, g2, m, wq, keys, tables)
            ffn_t = _peer(l, grp, h2t, routing, u, v)
            xs[grp] = _peer_residual(grp, ffn_t, x1, m, fg, final_norm=(l == DEPTH - 1))
            k_bufs[grp].append(k_buf)
            v_bufs[grp].append(v_buf)

    y_prompt = xs[prompt].reshape(x_prompt.shape)
    y_sample = xs[sample].reshape(n_s, sample.seq_rows, D)[:, :l_s]
    caches = lambda grp: (states[grp], jnp.stack(k_bufs[grp]), jnp.stack(v_bufs[grp]))
    return (y_prompt, y_sample) + caches(prompt) + caches(sample)
```
